```python
import math
import jax, jax.numpy as jnp
from jax import lax
import numpy as np


D_MODEL = 1024
BATCH = 8
SEQ = 4096
DEPTH = 1

CHUNK = 64
Q_BLOCK = 128
MEM_LEN = 256

DA_HEADS = 4
DA_HEAD_DIM = 64
DA_QK_WIDTH = DA_HEADS * 2 * DA_HEAD_DIM
DA_V_WIDTH = DA_HEADS * 2 * DA_HEAD_DIM

GLA_HEADS = 4
GLA_DK = 64
GLA_DV = 128
GLA_K_WIDTH = GLA_HEADS * GLA_DK
GLA_V_WIDTH = GLA_HEADS * GLA_DV
GLA_GATE_RANK = 16
GLA_GATE_NORM = 16.0

IN_SIZES = (DA_QK_WIDTH, DA_QK_WIDTH, DA_V_WIDTH,
            GLA_K_WIDTH, GLA_K_WIDTH, GLA_V_WIDTH, GLA_V_WIDTH, GLA_GATE_RANK)
IN_WIDTH = sum(IN_SIZES)
MIX_WIDTH = DA_V_WIDTH + GLA_V_WIDTH

FFN_HIDDEN = 2816
CROSS_HEADS = 4
CROSS_HEAD_DIM = D_MODEL // CROSS_HEADS

ALPHA = (2.0 * DEPTH) ** 0.25
BETA = (8.0 * DEPTH) ** -0.25
EPS = 1e-5

kernel_name = 'hymba_diffattn_gla_macaron_deepnorm'


def layer_norm(x, g, b):
    xf = x.astype(jnp.float32)
    mu = jnp.mean(xf, axis=-1, keepdims=True)
    var = jnp.mean(jnp.square(xf - mu), axis=-1, keepdims=True)
    y = (xf - mu) * lax.rsqrt(var + EPS)
    return (y * g.astype(jnp.float32) + b.astype(jnp.float32)).astype(x.dtype)


def rms_norm(x, g):
    xf = x.astype(jnp.float32)
    y = xf * lax.rsqrt(jnp.mean(jnp.square(xf), axis=-1, keepdims=True) + EPS)
    return (y * g.astype(jnp.float32)).astype(x.dtype)


def swiglu(x, w_gate, w_up, w_down):
    return (jax.nn.silu(x @ w_gate) * (x @ w_up)) @ w_down


def diff_attention(q, k, v, lam):
    B, S, H, _, d = q.shape
    n_qb = S // Q_BLOCK
    scale = DA_HEAD_DIM ** -0.5
    q_blocks = (q * scale).reshape(B, n_qb, Q_BLOCK, H, 2, d).transpose(1, 0, 3, 4, 2, 5)
    k_t = k.transpose(0, 2, 3, 1, 4)
    v_t = v.transpose(0, 2, 1, 3)
    key_chunk = jnp.arange(S) // CHUNK
    starts = jnp.arange(n_qb) * Q_BLOCK

    def one_block(args):
        q_blk, start = args
        s = jnp.einsum('bhcqd,bhckd->bhcqk', q_blk, k_t).astype(jnp.float32)
        q_chunk = (start + jnp.arange(Q_BLOCK)) // CHUNK
        mask = key_chunk[None, :] <= q_chunk[:, None]
        p = jax.nn.softmax(jnp.where(mask, s, -jnp.inf), axis=-1)
        w = p[:, :, 0] - lam * p[:, :, 1]
        return jnp.einsum('bhqk,bhkv->bhqv', w.astype(v_t.dtype), v_t)

    o = lax.map(one_block, (q_blocks, starts))
    return o.transpose(1, 0, 3, 2, 4).reshape(B, S, H, 2 * d)


def gla_chunk_causal(q, k, v, log_a):
    B, S, H, dk = q.shape
    dv = v.shape[-1]
    n_c = S // CHUNK
    qc = q.reshape(B, n_c, CHUNK, H, dk).astype(jnp.float32)
    kc = k.reshape(B, n_c, CHUNK, H, dk).astype(jnp.float32)
    vc = v.reshape(B, n_c, CHUNK, H, dv).astype(jnp.float32)
    cum = jnp.cumsum(log_a.reshape(B, n_c, CHUNK, H, dk).astype(jnp.float32), axis=2)
    total = cum[:, :, -1]
    k_end = kc * jnp.exp(total[:, :, None] - cum)
    d_state = jnp.einsum('bnchk,bnchv->nbhkv', k_end, vc)
    decay = jnp.exp(total).transpose(1, 0, 2, 3)

    def step(s_prev, inp):
        dec, ds = inp
        s_new = dec[..., None] * s_prev + ds
        return s_new, s_new

    _, states = lax.scan(step, jnp.zeros((B, H, dk, dv), jnp.float32), (decay, d_state))
    o = jnp.einsum('bnchk,nbhkv->bnchv', qc * (GLA_DK ** -0.5), states)
    return o.reshape(B, S, H, dv)


def memory_cross_attention(h, mem, w_q, w_kv, w_o):
    B, S, _ = h.shape
    M = mem.shape[1]
    q = (h @ w_q).reshape(B, S, CROSS_HEADS, CROSS_HEAD_DIM)
    k, v = jnp.split(mem @ w_kv, 2, axis=-1)
    k = k.reshape(B, M, CROSS_HEADS, CROSS_HEAD_DIM)
    v = v.reshape(B, M, CROSS_HEADS, CROSS_HEAD_DIM)
    s = jnp.einsum('bshd,bmhd->bhsm', q, k).astype(jnp.float32) * (CROSS_HEAD_DIM ** -0.5)
    p = jax.nn.softmax(s, axis=-1).astype(v.dtype)
    o = jnp.einsum('bhsm,bmhd->bshd', p, v).reshape(B, S, D_MODEL)
    return o @ w_o


def setup_inputs(seed: int = 0) -> dict:
    key = jax.random.key(seed)
    ks = iter(jax.random.split(key, 32))

    def nrm(shape, scale):
        return jax.random.normal(next(ks), shape, jnp.float32) * scale

    def gain(n):
        return 1.0 + nrm((DEPTH, n), 0.02)

    L, D, F = DEPTH, D_MODEL, FFN_HIDDEN
    return {
        'x': nrm((BATCH, SEQ, D), 1.0),
        'mem': nrm((BATCH, MEM_LEN, D), 1.0),
        'ffn1_w_gate': nrm((L, D, F), D ** -0.5),
        'ffn1_w_up': nrm((L, D, F), D ** -0.5),
        'ffn1_w_down': nrm((L, F, D), BETA * F ** -0.5),
        'ln1_g': gain(D),
        'ln1_b': nrm((L, D), 0.02),
        'w_in': nrm((L, D, IN_WIDTH), D ** -0.5),
        'da_lambda_q1': nrm((L, DA_HEAD_DIM), 0.1),
        'da_lambda_k1': nrm((L, DA_HEAD_DIM), 0.1),
        'da_lambda_q2': nrm((L, DA_HEAD_DIM), 0.1),
        'da_lambda_k2': nrm((L, DA_HEAD_DIM), 0.1),
        'da_norm_g': gain(2 * DA_HEAD_DIM),
        'gla_w_gate2': nrm((L, GLA_GATE_RANK, GLA_K_WIDTH), GLA_GATE_RANK ** -0.5),
        'gla_b_gate': nrm((L, GLA_K_WIDTH), 0.02),
        'gla_norm_g': gain(GLA_DV),
        'w_out': nrm((L, MIX_WIDTH, D), BETA * MIX_WIDTH ** -0.5),
        'ln2_g': gain(D),
        'ln2_b': nrm((L, D), 0.02),
        'cross_wq': nrm((L, D, D), D ** -0.5),
        'cross_wkv': nrm((L, D, 2 * D), D ** -0.5),
        'cross_wo': nrm((L, D, D), BETA * D ** -0.5),
        'ln3_g': gain(D),
        'ln3_b': nrm((L, D), 0.02),
        'ffn2_w_gate': nrm((L, D, F), D ** -0.5),
        'ffn2_w_up': nrm((L, D, F), D ** -0.5),
        'ffn2_w_down': nrm((L, F, D), BETA * F ** -0.5),
        'ln4_g': gain(D),
        'ln4_b': nrm((L, D), 0.02),
    }


def reference(x, mem, ffn1_w_gate, ffn1_w_up, ffn1_w_down, ln1_g, ln1_b, w_in,
              da_lambda_q1, da_lambda_k1, da_lambda_q2, da_lambda_k2, da_norm_g,
              gla_w_gate2, gla_b_gate, gla_norm_g, w_out, ln2_g, ln2_b,
              cross_wq, cross_wkv, cross_wo, ln3_g, ln3_b,
              ffn2_w_gate, ffn2_w_up, ffn2_w_down, ln4_g, ln4_b):
    B, S, _ = x.shape
    split_points = [int(i) for i in np.cumsum(IN_SIZES)[:-1]]
    h = x
    for l in range(DEPTH):
        h = layer_norm(ALPHA * h + 0.5 * swiglu(h, ffn1_w_gate[l], ffn1_w_up[l], ffn1_w_down[l]),
                       ln1_g[l], ln1_b[l])

        q_da, k_da, v_da, q_g, k_g, v_g, r_g, g_lr = jnp.split(h @ w_in[l], split_points, axis=-1)

        lambda_init = 0.8 - 0.6 * math.exp(-0.3 * l)
        lam = (jnp.exp(jnp.sum(da_lambda_q1[l].astype(jnp.float32) * da_lambda_k1[l].astype(jnp.float32)))
               - jnp.exp(jnp.sum(da_lambda_q2[l].astype(jnp.float32) * da_lambda_k2[l].astype(jnp.float32)))
               + lambda_init)
        o_da = diff_attention(q_da.reshape(B, S, DA_HEADS, 2, DA_HEAD_DIM),
                              k_da.reshape(B, S, DA_HEADS, 2, DA_HEAD_DIM),
                              v_da.reshape(B, S, DA_HEADS, 2 * DA_HEAD_DIM), lam)
        o_da = rms_norm(o_da, da_norm_g[l]) * (1.0 - lambda_init)

        log_a = jax.nn.log_sigmoid((g_lr @ gla_w_gate2[l] + gla_b_gate[l]).astype(jnp.float32)) / GLA_GATE_NORM
        o_g = gla_chunk_causal(q_g.reshape(B, S, GLA_HEADS, GLA_DK),
                               k_g.reshape(B, S, GLA_HEADS, GLA_DK),
                               v_g.reshape(B, S, GLA_HEADS, GLA_DV),
                               log_a.reshape(B, S, GLA_HEADS, GLA_DK))
        o_g = rms_norm(o_g.astype(x.dtype), gla_norm_g[l]) * jax.nn.silu(r_g).reshape(B, S, GLA_HEADS, GLA_DV)

        mix = jnp.concatenate([o_da.reshape(B, S, DA_V_WIDTH).astype(x.dtype),
                               o_g.reshape(B, S, GLA_V_WIDTH).astype(x.dtype)], axis=-1) @ w_out[l]
        h = layer_norm(ALPHA * h + mix, ln2_g[l], ln2_b[l])

        c = memory_cross_attention(h, mem, cross_wq[l], cross_wkv[l], cross_wo[l])
        h = layer_norm(ALPHA * h + c, ln3_g[l], ln3_b[l])

        h = layer_norm(ALPHA * h + 0.5 * swiglu(h, ffn2_w_gate[l], ffn2_w_up[l], ffn2_w_down[l]),
                       ln4_g[l], ln4_b[l])
    return h
```

```python
import functools
import math

import jax
import jax.numpy as jnp
from jax import lax
from jax.experimental import pallas as pl
from jax.experimental.pallas import tpu as pltpu

D_MODEL = 1024
CHUNK = 64
DA_HEADS = 4
DA_HEAD_DIM = 64
DA_WIDTH = DA_HEADS * 2 * DA_HEAD_DIM
GLA_HEADS = 4
GLA_DK = 64
GLA_DV = 128
GLA_K_WIDTH = GLA_HEADS * GLA_DK
GLA_V_WIDTH = GLA_HEADS * GLA_DV
GLA_GATE_RANK = 16
GLA_GATE_NORM = 16.0
FFN_HIDDEN = 2816
CROSS_HEADS = 4
CROSS_HEAD_DIM = D_MODEL // CROSS_HEADS
DEPTH = 1
ALPHA = (2.0 * DEPTH) ** 0.25
EPS = 1e-5
LANES = 128

_OFF_DA = 0
_OFF_GLA = 3 * DA_WIDTH
_GLA_MAIN = 2 * GLA_K_WIDTH + 2 * GLA_V_WIDTH
_OFF_GATE = _OFF_GLA + _GLA_MAIN

TOKEN_TILE = 512
FFN_CHUNKS = 2
DA_TQ = 256
DA_TK = 256
VMEM_LIMIT = 56 * 1024 * 1024

F32 = jnp.float32
BF16 = jnp.bfloat16
_NEG = -1e30

_NT = (((1,), (1,)), ((), ()))
_TN = (((0,), (0,)), ((), ()))


def _dot(a, b):
    return jnp.dot(a, b, preferred_element_type=F32)


def _split_bf16(a):
    hi = a.astype(BF16)
    lo = (a - hi.astype(F32)).astype(BF16)
    return hi, lo


def _layer_norm(z, g, b):
    mu = jnp.mean(z, axis=-1, keepdims=True)
    zc = z - mu
    var = jnp.mean(zc * zc, axis=-1, keepdims=True)
    return zc * lax.rsqrt(var + EPS) * g + b


def _const_spec(shape):
    nd = len(shape)
    return pl.BlockSpec(shape, lambda *_: (0,) * nd, pipeline_mode=pl.Buffered(1))


def _params(*sem):
    return pltpu.CompilerParams(dimension_semantics=sem, vmem_limit_bytes=VMEM_LIMIT)


def _ffn_ln_kernel(x_ref, wg_ref, wu_ref, wd_ref, g_ref, b_ref, o_ref):
    x = x_ref[...]
    xb = x.astype(BF16)
    fc = FFN_HIDDEN // FFN_CHUNKS
    y = None
    for c in range(FFN_CHUNKS):
        cols = slice(c * fc, (c + 1) * fc)
        gate = _dot(xb, wg_ref[:, cols])
        up = _dot(xb, wu_ref[:, cols])
        hid = (gate * jax.nn.sigmoid(gate) * up).astype(BF16)
        part = _dot(hid, wd_ref[cols, :])
        y = part if y is None else y + part
    o_ref[...] = _layer_norm(ALPHA * x + 0.5 * y, g_ref[...], b_ref[...])


def _ffn_ln(x, wg, wu, wd, g, b):
    n = x.shape[0]
    tm = TOKEN_TILE
    row = pl.BlockSpec((tm, D_MODEL), lambda i: (i, 0))
    return pl.pallas_call(
        _ffn_ln_kernel,
        grid=(n // tm,),
        in_specs=[row, _const_spec(wg.shape), _const_spec(wu.shape), _const_spec(wd.shape),
                  _const_spec(g.shape), _const_spec(b.shape)],
        out_specs=row,
        out_shape=jax.ShapeDtypeStruct((n, D_MODEL), F32),
        compiler_params=_params("parallel"),
        name="ffn_ln",
    )(x, wg, wu, wd, g, b)


def _in_proj_kernel(h_ref, wda_ref, wgla_ref, wlr_ref, wg2_ref, bg_ref,
                    da_ref, vt_ref, gla_ref, kend_ref, dec_ref):
    tm = h_ref.shape[0]
    hb = h_ref[...].astype(BF16)

    y_da = _dot(hb, wda_ref[...])
    da_ref[...] = y_da[:, :2 * DA_WIDTH].astype(BF16)
    v_t = y_da[:, 2 * DA_WIDTH:].T
    for h in range(DA_HEADS):
        for j in range(tm // DA_TK):
            vt_ref[0, h, j] = v_t[h * 2 * DA_HEAD_DIM:(h + 1) * 2 * DA_HEAD_DIM,
                                  j * DA_TK:(j + 1) * DA_TK].astype(BF16)

    y_g = _dot(hb, wgla_ref[...])
    gla_ref[...] = y_g.astype(BF16)

    g_lr = _dot(hb, wlr_ref[...])
    pre = _dot(g_lr.astype(BF16), wg2_ref[...]) + bg_ref[...]
    log_a = jax.nn.log_sigmoid(pre) / GLA_GATE_NORM

    r = lax.broadcasted_iota(jnp.int32, (tm, tm), 0)
    c = lax.broadcasted_iota(jnp.int32, (tm, tm), 1)
    later = jnp.where((c > r) & ((c // CHUNK) == (r // CHUNK)), 1.0, 0.0).astype(BF16)
    la_hi, la_lo = _split_bf16(log_a)
    rev = _dot(later, la_hi) + _dot(later, la_lo)
    k_g = y_g[:, GLA_K_WIDTH:2 * GLA_K_WIDTH]
    kend_ref[...] = (k_g * jnp.exp(rev)).astype(BF16)
    total = jnp.sum(log_a.reshape(tm // CHUNK, CHUNK, GLA_K_WIDTH), axis=1)
    dec_ref[...] = jnp.exp(total)


def _in_proj(h, w_da, w_gla, w_lr, w_g2, b_g, batch, seq):
    n = h.shape[0]
    tm = TOKEN_TILE
    per_b = seq // tm
    row = lambda w: pl.BlockSpec((tm, w), lambda i: (i, 0))
    return pl.pallas_call(
        _in_proj_kernel,
        grid=(n // tm,),
        in_specs=[row(D_MODEL), _const_spec(w_da.shape), _const_spec(w_gla.shape),
                  _const_spec(w_lr.shape), _const_spec(w_g2.shape), _const_spec(b_g.shape)],
        out_specs=[
            row(2 * DA_WIDTH),
            pl.BlockSpec((1, DA_HEADS, tm // DA_TK, 2 * DA_HEAD_DIM, DA_TK),
                         lambda i: (i // per_b, 0, i % per_b, 0, 0)),
            row(_GLA_MAIN),
            row(GLA_K_WIDTH),
            pl.BlockSpec((tm // CHUNK, GLA_K_WIDTH), lambda i: (i, 0)),
        ],
        out_shape=[
            jax.ShapeDtypeStruct((n, 2 * DA_WIDTH), BF16),
            jax.ShapeDtypeStruct((batch, DA_HEADS, seq // DA_TK, 2 * DA_HEAD_DIM, DA_TK), BF16),
            jax.ShapeDtypeStruct((n, _GLA_MAIN), BF16),
            jax.ShapeDtypeStruct((n, GLA_K_WIDTH), BF16),
            jax.ShapeDtypeStruct((n // CHUNK, GLA_K_WIDTH), F32),
        ],
        compiler_params=_params("parallel"),
        name="in_proj",
    )(h, w_da, w_gla, w_lr, w_g2, b_g)


def _diff_attn_kernel(lq1_ref, lk1_ref, lq2_ref, lk2_ref, g_ref, q_ref, k_ref, vt_ref,
                      o_ref, acc_ref, *, lambda_init):
    i = pl.program_id(2)
    tq, tk = DA_TQ, DA_TK

    q = q_ref[0] * (DA_HEAD_DIM ** -0.5)
    lane = lax.broadcasted_iota(jnp.int32, q.shape, 1)
    q_halves = (jnp.where(lane < DA_HEAD_DIM, q, 0).astype(BF16),
                jnp.where(lane >= DA_HEAD_DIM, q, 0).astype(BF16))

    acc_ref[...] = jnp.zeros_like(acc_ref)

    def block(j, carry, masked):
        kj = k_ref[0, pl.ds(pl.multiple_of(j * tk, tk), tk), :]
        vtj = vt_ref[0, 0, j]
        if masked:
            kr = lax.broadcasted_iota(jnp.int32, (tk, tq), 0) // CHUNK
            qc = lax.broadcasted_iota(jnp.int32, (tk, tq), 1) // CHUNK
            keep = kr <= qc
        out = []
        for c in range(2):
            m_old, l_old = carry[c]
            s = lax.dot_general(kj, q_halves[c], _NT, preferred_element_type=F32)
            if masked:
                s = jnp.where(keep, s, _NEG)
            m_new = jnp.maximum(m_old, jnp.max(s, axis=0, keepdims=True))
            p = jnp.exp(s - m_new)
            a = jnp.exp(m_old - m_new)
            l_new = a * l_old + jnp.sum(p, axis=0, keepdims=True)
            acc_ref[c] = a * acc_ref[c] + _dot(vtj, p.astype(BF16))
            out.append((m_new, l_new))
        return tuple(out)

    init = tuple((jnp.full((1, tq), _NEG, F32), jnp.zeros((1, tq), F32)) for _ in range(2))
    carry = lax.fori_loop(0, i, functools.partial(block, masked=False), init)
    (_, l0), (_, l1) = block(i, carry, masked=True)

    lam = (jnp.exp(jnp.sum(lq1_ref[...] * lk1_ref[...], axis=-1, keepdims=True))
           - jnp.exp(jnp.sum(lq2_ref[...] * lk2_ref[...], axis=-1, keepdims=True))
           + lambda_init)
    o = acc_ref[0] / l0 - lam * (acc_ref[1] / l1)
    ms = jnp.mean(o * o, axis=0, keepdims=True)
    y = o * lax.rsqrt(ms + EPS) * g_ref[...] * (1.0 - lambda_init)
    o_ref[0] = y.T.astype(BF16)


def _diff_attn(da, vt, lq1, lk1, lq2, lk2, g_col, lambda_init):
    batch, seq, _ = da.shape
    hd2 = 2 * DA_HEAD_DIM
    lam_spec = _const_spec(lq1.shape)
    return pl.pallas_call(
        functools.partial(_diff_attn_kernel, lambda_init=lambda_init),
        grid=(batch, DA_HEADS, seq // DA_TQ),
        in_specs=[lam_spec, lam_spec, lam_spec, lam_spec, _const_spec(g_col.shape),
                  pl.BlockSpec((1, DA_TQ, hd2), lambda b, h, i: (b, i, h)),
                  pl.BlockSpec((1, seq, hd2), lambda b, h, i: (b, 0, DA_HEADS + h)),
                  pl.BlockSpec((1, 1, seq // DA_TK, hd2, DA_TK), lambda b, h, i: (b, h, 0, 0, 0))],
        out_specs=pl.BlockSpec((1, DA_TQ, hd2), lambda b, h, i: (b, i, h)),
        out_shape=jax.ShapeDtypeStruct((batch, seq, DA_WIDTH), BF16),
        scratch_shapes=[pltpu.VMEM((2, hd2, DA_TQ), F32)],
        compiler_params=_params("parallel", "parallel", "parallel"),
        name="diff_attn",
    )(lq1, lk1, lq2, lk2, g_col, da, da, vt)


def _gla_kernel(g_ref, main_ref, kend_ref, dec_ref, o_ref, st_ref, stb_ref, of_ref):
    @pl.when(pl.program_id(1) == 0)
    def _():
        st_ref[...] = jnp.zeros_like(st_ref)
        stb_ref[...] = jnp.zeros_like(stb_ref)

    tm = main_ref.shape[0]
    for c in range(tm // CHUNK):
        rows = slice(c * CHUNK, (c + 1) * CHUNK)
        q = main_ref[rows, :GLA_K_WIDTH] * (GLA_DK ** -0.5)
        v = main_ref[rows, 2 * GLA_K_WIDTH:2 * GLA_K_WIDTH + GLA_V_WIDTH]
        upd = lax.dot_general(v, kend_ref[rows, :], _TN, preferred_element_type=F32)
        dec = dec_ref[c:c + 1, :]
        for h in range(GLA_HEADS):
            vr = slice(h * GLA_DV, (h + 1) * GLA_DV)
            kc = slice(h * GLA_DK, (h + 1) * GLA_DK)
            new = st_ref[vr, kc] * dec[:, kc] + upd[vr, kc]
            st_ref[vr, kc] = new
            stb_ref[vr, kc] = new.astype(BF16)
        of_ref[rows, :] = lax.dot_general(q.astype(BF16), stb_ref[...], _NT,
                                          preferred_element_type=F32)

    r = main_ref[:, 2 * GLA_K_WIDTH + GLA_V_WIDTH:].astype(F32)
    gate = r * jax.nn.sigmoid(r)
    for h in range(GLA_HEADS):
        vr = slice(h * GLA_DV, (h + 1) * GLA_DV)
        o = of_ref[:, vr]
        ms = jnp.mean(o * o, axis=-1, keepdims=True)
        o_ref[:, vr] = (o * lax.rsqrt(ms + EPS) * g_ref[...] * gate[:, vr]).astype(BF16)


def _gla(main, kend, dec, g_row, batch, seq):
    n = main.shape[0]
    tm = TOKEN_TILE
    per_b = seq // tm
    idx = lambda b, i: (b * per_b + i, 0)
    return pl.pallas_call(
        _gla_kernel,
        grid=(batch, per_b),
        in_specs=[_const_spec(g_row.shape),
                  pl.BlockSpec((tm, _GLA_MAIN), idx),
                  pl.BlockSpec((tm, GLA_K_WIDTH), idx),
                  pl.BlockSpec((tm // CHUNK, GLA_K_WIDTH), idx)],
        out_specs=pl.BlockSpec((tm, GLA_V_WIDTH), idx),
        out_shape=jax.ShapeDtypeStruct((n, GLA_V_WIDTH), BF16),
        scratch_shapes=[pltpu.VMEM((GLA_V_WIDTH, GLA_K_WIDTH), F32),
                        pltpu.VMEM((GLA_V_WIDTH, GLA_K_WIDTH), BF16),
                        pltpu.VMEM((tm, GLA_V_WIDTH), F32)],
        compiler_params=_params("parallel", "arbitrary"),
        name="gla",
    )(g_row, main, kend, dec)


def _mem_kv_kernel(m_ref, w_ref, k_ref, v_ref):
    y = _dot(m_ref[0].astype(BF16), w_ref[...])
    k_ref[0] = y[:, :D_MODEL].astype(BF16)
    v_ref[0] = y[:, D_MODEL:].astype(BF16)


def _mem_kv(mem, wkv):
    batch, m, _ = mem.shape
    blk = pl.BlockSpec((1, m, D_MODEL), lambda b: (b, 0, 0))
    out = jax.ShapeDtypeStruct((batch, m, D_MODEL), BF16)
    return pl.pallas_call(
        _mem_kv_kernel,
        grid=(batch,),
        in_specs=[blk, _const_spec(wkv.shape)],
        out_specs=[blk, blk],
        out_shape=[out, out],
        compiler_params=_params("parallel"),
        name="mem_kv",
    )(mem, wkv)


def _mix_cross_kernel(oda_ref, og_ref, h_ref, wo1_ref, wo2_ref, g2_ref, b2_ref,
                      wq_ref, k_ref, v_ref, wo_ref, g3_ref, b3_ref, o_ref):
    mix = _dot(oda_ref[...], wo1_ref[...]) + _dot(og_ref[...], wo2_ref[...])
    h2 = _layer_norm(ALPHA * h_ref[...] + mix, g2_ref[...], b2_ref[...])

    q = (_dot(h2.astype(BF16), wq_ref[...]) * (CROSS_HEAD_DIM ** -0.5)).astype(BF16)
    heads = []
    for h in range(CROSS_HEADS):
        cols = slice(h * CROSS_HEAD_DIM, (h + 1) * CROSS_HEAD_DIM)
        s = lax.dot_general(q[:, cols], k_ref[0, :, cols], _NT, preferred_element_type=F32)
        p = jnp.exp(s - jnp.max(s, axis=-1, keepdims=True))
        l = jnp.sum(p, axis=-1, keepdims=True)
        heads.append((_dot(p.astype(BF16), v_ref[0, :, cols]) / l).astype(BF16))
    c = _dot(jnp.concatenate(heads, axis=-1), wo_ref[...])
    o_ref[...] = _layer_norm(ALPHA * h2 + c, g3_ref[...], b3_ref[...])


def _mix_cross(o_da, o_g, h1, wo1, wo2, g2, b2, wq, k_mem, v_mem, wo, g3, b3, seq):
    n = h1.shape[0]
    tm = TOKEN_TILE
    per_b = seq // tm
    row = lambda w: pl.BlockSpec((tm, w), lambda i: (i, 0))
    kv = pl.BlockSpec((1,) + k_mem.shape[1:], lambda i: (i // per_b, 0, 0))
    return pl.pallas_call(
        _mix_cross_kernel,
        grid=(n // tm,),
        in_specs=[row(DA_WIDTH), row(GLA_V_WIDTH), row(D_MODEL),
                  _const_spec(wo1.shape), _const_spec(wo2.shape),
                  _const_spec(g2.shape), _const_spec(b2.shape),
                  _const_spec(wq.shape), kv, kv, _const_spec(wo.shape),
                  _const_spec(g3.shape), _const_spec(b3.shape)],
        out_specs=row(D_MODEL),
        out_shape=jax.ShapeDtypeStruct((n, D_MODEL), F32),
        compiler_params=_params("parallel"),
        name="mix_cross",
    )(o_da, o_g, h1, wo1, wo2, g2, b2, wq, k_mem, v_mem, wo, g3, b3)


def kernel(x, mem, ffn1_w_gate, ffn1_w_up, ffn1_w_down, ln1_g, ln1_b, w_in, da_lambda_q1, da_lambda_k1, da_lambda_q2, da_lambda_k2, da_norm_g, gla_w_gate2, gla_b_gate, gla_norm_g, w_out, ln2_g, ln2_b, cross_wq, cross_wkv, cross_wo, ln3_g, ln3_b, ffn2_w_gate, ffn2_w_up, ffn2_w_down, ln4_g, ln4_b):
    batch, seq, d = x.shape
    assert d == D_MODEL and seq % TOKEN_TILE == 0 and TOKEN_TILE % DA_TK == 0 and DA_TQ == DA_TK
    n = batch * seq
    h = x.reshape(n, d)
    row = lambda a, l: a[l].reshape(1, -1)
    for l in range(DEPTH):
        bf = lambda w: w[l].astype(BF16)
        lambda_init = 0.8 - 0.6 * math.exp(-0.3 * l)
        w_in_l = w_in[l]
        pad = LANES - GLA_GATE_RANK
        w_lr = jnp.pad(w_in_l[:, _OFF_GATE:], ((0, 0), (0, pad))).astype(BF16)
        w_g2 = jnp.pad(gla_w_gate2[l], ((0, pad), (0, 0))).astype(BF16)

        h1 = _ffn_ln(h, bf(ffn1_w_gate), bf(ffn1_w_up), bf(ffn1_w_down), row(ln1_g, l), row(ln1_b, l))
        da, vt, gla_main, kend, dec = _in_proj(
            h1, w_in_l[:, _OFF_DA:_OFF_GLA].astype(BF16), w_in_l[:, _OFF_GLA:_OFF_GATE].astype(BF16),
            w_lr, w_g2, row(gla_b_gate, l), batch, seq)
        o_da = _diff_attn(da.reshape(batch, seq, 2 * DA_WIDTH), vt,
                          row(da_lambda_q1, l), row(da_lambda_k1, l),
                          row(da_lambda_q2, l), row(da_lambda_k2, l),
                          da_norm_g[l].reshape(-1, 1), lambda_init)
        o_g = _gla(gla_main, kend, dec, row(gla_norm_g, l), batch, seq)
        k_mem, v_mem = _mem_kv(mem, bf(cross_wkv))
        w_out_l = bf(w_out)
        h3 = _mix_cross(o_da.reshape(n, DA_WIDTH), o_g, h1, w_out_l[:DA_WIDTH], w_out_l[DA_WIDTH:],
                        row(ln2_g, l), row(ln2_b, l), bf(cross_wq), k_mem, v_mem, bf(cross_wo),
                        row(ln3_g, l), row(ln3_b, l), seq)
        h = _ffn_ln(h3, bf(ffn2_w_gate), bf(ffn2_w_up), bf(ffn2_w_down), row(ln4_g, l), row(ln4_b, l))
    return h.reshape(batch, seq, d)
```

```python
import functools
import math

import jax
import jax.numpy as jnp
from jax import lax
from jax.experimental import pallas as pl
from jax.experimental.pallas import tpu as pltpu

D_MODEL = 1024
CHUNK = 64
DA_HEADS = 4
DA_HEAD_DIM = 64
DA_WIDTH = DA_HEADS * 2 * DA_HEAD_DIM
GLA_HEADS = 4
GLA_DK = 64
GLA_DV = 128
GLA_K_WIDTH = GLA_HEADS * GLA_DK
GLA_V_WIDTH = GLA_HEADS * GLA_DV
GLA_GATE_RANK = 16
GLA_GATE_NORM = 16.0
FFN_HIDDEN = 2816
CROSS_HEADS = 4
CROSS_HEAD_DIM = D_MODEL // CROSS_HEADS
DEPTH = 1
ALPHA = (2.0 * DEPTH) ** 0.25
EPS = 1e-5
LANES = 128

_OFF_DA = 0
_OFF_GLA = 3 * DA_WIDTH
_GLA_MAIN = 2 * GLA_K_WIDTH + 2 * GLA_V_WIDTH
_OFF_GATE = _OFF_GLA + _GLA_MAIN

TOKEN_TILE = 512
FFN_CHUNKS = 2
DA_T = 256
VMEM_LIMIT = 56 * 1024 * 1024

F32 = jnp.float32
BF16 = jnp.bfloat16
_NEG = -1e30
_DA_Q_SCALE = DA_HEAD_DIM ** -0.5 * math.log2(math.e)

_NT = (((1,), (1,)), ((), ()))
_TN = (((0,), (0,)), ((), ()))


def _dot(a, b):
    return jnp.dot(a, b, preferred_element_type=F32)


def _split_bf16(a):
    hi = a.astype(BF16)
    lo = (a - hi.astype(F32)).astype(BF16)
    return hi, lo


def _layer_norm(z, g, b):
    mu = jnp.mean(z, axis=-1, keepdims=True)
    zc = z - mu
    var = jnp.mean(zc * zc, axis=-1, keepdims=True)
    return zc * lax.rsqrt(var + EPS) * g + b


def _const_spec(shape):
    nd = len(shape)
    return pl.BlockSpec(shape, lambda *_: (0,) * nd, pipeline_mode=pl.Buffered(1))


def _params(*sem):
    return pltpu.CompilerParams(dimension_semantics=sem, vmem_limit_bytes=VMEM_LIMIT)


def _ffn_ln_kernel(x_ref, wg_ref, wu_ref, wd_ref, g_ref, b_ref, o_ref):
    x = x_ref[...]
    xb = x.astype(BF16)
    fc = FFN_HIDDEN // FFN_CHUNKS
    y = None
    for c in range(FFN_CHUNKS):
        cols = slice(c * fc, (c + 1) * fc)
        gate = _dot(xb, wg_ref[:, cols])
        up = _dot(xb, wu_ref[:, cols])
        hid = (gate * jax.nn.sigmoid(gate) * up).astype(BF16)
        part = _dot(hid, wd_ref[cols, :])
        y = part if y is None else y + part
    o_ref[...] = _layer_norm(ALPHA * x + 0.5 * y, g_ref[...], b_ref[...])


def _ffn_ln(x, wg, wu, wd, g, b):
    n = x.shape[0]
    tm = TOKEN_TILE
    row = pl.BlockSpec((tm, D_MODEL), lambda i: (i, 0))
    return pl.pallas_call(
        _ffn_ln_kernel,
        grid=(n // tm,),
        in_specs=[row, _const_spec(wg.shape), _const_spec(wu.shape), _const_spec(wd.shape),
                  _const_spec(g.shape), _const_spec(b.shape)],
        out_specs=row,
        out_shape=jax.ShapeDtypeStruct((n, D_MODEL), F32),
        compiler_params=_params("parallel"),
        name="ffn_ln",
    )(x, wg, wu, wd, g, b)


def _in_proj_kernel(h_ref, wda_ref, wgla_ref, wlr_ref, wg2_ref, bg_ref,
                    da_ref, vt_ref, gla_ref, kend_ref, dec_ref):
    tm = h_ref.shape[0]
    hb = h_ref[...].astype(BF16)

    y_da = _dot(hb, wda_ref[...])
    da_ref[:, :DA_WIDTH] = (y_da[:, :DA_WIDTH] * _DA_Q_SCALE).astype(BF16)
    da_ref[:, DA_WIDTH:] = y_da[:, DA_WIDTH:2 * DA_WIDTH].astype(BF16)
    v_t = y_da[:, 2 * DA_WIDTH:].T
    for h in range(DA_HEADS):
        for j in range(tm // DA_T):
            vt_ref[0, h, j] = v_t[h * 2 * DA_HEAD_DIM:(h + 1) * 2 * DA_HEAD_DIM,
                                  j * DA_T:(j + 1) * DA_T].astype(BF16)

    y_g = _dot(hb, wgla_ref[...])
    gla_ref[...] = y_g.astype(BF16)

    g_lr = _dot(hb, wlr_ref[...])
    pre = _dot(g_lr.astype(BF16), wg2_ref[...]) + bg_ref[...]
    log_a = jax.nn.log_sigmoid(pre) / GLA_GATE_NORM

    r = lax.broadcasted_iota(jnp.int32, (tm, tm), 0)
    c = lax.broadcasted_iota(jnp.int32, (tm, tm), 1)
    later = jnp.where((c > r) & ((c // CHUNK) == (r // CHUNK)), 1.0, 0.0).astype(BF16)
    la_hi, la_lo = _split_bf16(log_a)
    rev = _dot(later, la_hi) + _dot(later, la_lo)
    k_g = y_g[:, GLA_K_WIDTH:2 * GLA_K_WIDTH]
    kend_ref[...] = (k_g * jnp.exp(rev)).astype(BF16)
    total = jnp.sum(log_a.reshape(tm // CHUNK, CHUNK, GLA_K_WIDTH), axis=1)
    dec_ref[...] = jnp.exp(total)


def _in_proj(h, w_da, w_gla, w_lr, w_g2, b_g, batch, seq):
    n = h.shape[0]
    tm = TOKEN_TILE
    per_b = seq // tm
    row = lambda w: pl.BlockSpec((tm, w), lambda i: (i, 0))
    return pl.pallas_call(
        _in_proj_kernel,
        grid=(n // tm,),
        in_specs=[row(D_MODEL), _const_spec(w_da.shape), _const_spec(w_gla.shape),
                  _const_spec(w_lr.shape), _const_spec(w_g2.shape), _const_spec(b_g.shape)],
        out_specs=[
            row(2 * DA_WIDTH),
            pl.BlockSpec((1, DA_HEADS, tm // DA_T, 2 * DA_HEAD_DIM, DA_T),
                         lambda i: (i // per_b, 0, i % per_b, 0, 0)),
            row(_GLA_MAIN),
            row(GLA_K_WIDTH),
            pl.BlockSpec((tm // CHUNK, GLA_K_WIDTH), lambda i: (i, 0)),
        ],
        out_shape=[
            jax.ShapeDtypeStruct((n, 2 * DA_WIDTH), BF16),
            jax.ShapeDtypeStruct((batch, DA_HEADS, seq // DA_T, 2 * DA_HEAD_DIM, DA_T), BF16),
            jax.ShapeDtypeStruct((n, _GLA_MAIN), BF16),
            jax.ShapeDtypeStruct((n, GLA_K_WIDTH), BF16),
            jax.ShapeDtypeStruct((n // CHUNK, GLA_K_WIDTH), F32),
        ],
        compiler_params=_params("parallel"),
        name="in_proj",
    )(h, w_da, w_gla, w_lr, w_g2, b_g)


def _diff_attn_kernel(lq1_ref, lk1_ref, lq2_ref, lk2_ref, g_ref, q_ref, k_ref, vt_ref,
                      o_ref, s_a, s_b, p_a, p_b, acc_ref, l_ref, *, lambda_init):
    i = pl.program_id(2)
    t = DA_T

    qt = q_ref[0].astype(F32).T
    row = lax.broadcasted_iota(jnp.int32, qt.shape, 0)
    qt_halves = (jnp.where(row < DA_HEAD_DIM, qt, 0.0).astype(BF16),
                 jnp.where(row >= DA_HEAD_DIM, qt, 0.0).astype(BF16))

    def scores(j, s_ref):
        kj = k_ref[0, pl.ds(pl.multiple_of(j * t, t), t), :]
        for c in range(2):
            s_ref[c] = _dot(kj, qt_halves[c])

    def softmax(s_ref, p_ref, m, l, masked):
        if masked:
            kc = lax.broadcasted_iota(jnp.int32, (t, t), 0) // CHUNK
            qc = lax.broadcasted_iota(jnp.int32, (t, t), 1) // CHUNK
            keep = kc <= qc
        m_out, l_out, a_out = [], [], []
        for c in range(2):
            s = s_ref[c]
            if masked:
                s = jnp.where(keep, s, _NEG)
            m_new = jnp.maximum(m[c], jnp.max(s, axis=0, keepdims=True))
            p = jnp.exp2(s - m_new)
            a = jnp.exp2(m[c] - m_new)
            p_ref[c] = p.astype(BF16)
            m_out.append(m_new)
            l_out.append(a * l[c] + jnp.sum(p, axis=0, keepdims=True))
            a_out.append(a)
        return tuple(m_out), tuple(l_out), tuple(a_out)

    def values(j, p_ref, a):
        vtj = vt_ref[0, 0, j]
        for c in range(2):
            acc_ref[c] = a[c] * acc_ref[c] + _dot(vtj, p_ref[c])

    def finish(l):
        for c in range(2):
            l_ref[c] = l[c]

    acc_ref[...] = jnp.zeros_like(acc_ref)
    p_b[...] = jnp.zeros_like(p_b)
    neg = jnp.full((1, t), _NEG, F32)
    zero = jnp.zeros((1, t), F32)
    one = jnp.ones((1, t), F32)
    scores(0, s_a)

    def pair(jj, carry):
        m, l, a_prev = carry
        j0 = 2 * jj
        scores(j0 + 1, s_b)
        m, l, a0 = softmax(s_a, p_a, m, l, masked=False)
        values(jnp.maximum(j0 - 1, 0), p_b, a_prev)
        scores(j0 + 2, s_a)
        m, l, a1 = softmax(s_b, p_b, m, l, masked=False)
        values(j0, p_a, a0)
        return m, l, a1

    m, l, a_prev = lax.fori_loop(0, i // 2, pair, ((neg, neg), (zero, zero), (one, one)))

    @pl.when(i % 2 == 0)
    def _():
        _, l1, a = softmax(s_a, p_a, m, l, masked=True)
        values(jnp.maximum(i - 1, 0), p_b, a_prev)
        values(i, p_a, a)
        finish(l1)

    @pl.when(i % 2 == 1)
    def _():
        scores(i, s_b)
        m1, l1, a0 = softmax(s_a, p_a, m, l, masked=False)
        values(jnp.maximum(i - 2, 0), p_b, a_prev)
        _, l2, a1 = softmax(s_b, p_b, m1, l1, masked=True)
        values(i - 1, p_a, a0)
        values(i, p_b, a1)
        finish(l2)

    lam = (jnp.exp(jnp.sum(lq1_ref[...] * lk1_ref[...], axis=-1, keepdims=True))
           - jnp.exp(jnp.sum(lq2_ref[...] * lk2_ref[...], axis=-1, keepdims=True))
           + lambda_init)
    o = acc_ref[0] / l_ref[0] - lam * (acc_ref[1] / l_ref[1])
    ms = jnp.mean(o * o, axis=0, keepdims=True)
    y = o * lax.rsqrt(ms + EPS) * g_ref[...] * (1.0 - lambda_init)
    o_ref[0] = y.T.astype(BF16)


def _diff_attn(da, vt, lq1, lk1, lq2, lk2, g_col, lambda_init):
    batch, seq, _ = da.shape
    hd2 = 2 * DA_HEAD_DIM
    t = DA_T
    lam_spec = _const_spec(lq1.shape)
    return pl.pallas_call(
        functools.partial(_diff_attn_kernel, lambda_init=lambda_init),
        grid=(batch, DA_HEADS, seq // t),
        in_specs=[lam_spec, lam_spec, lam_spec, lam_spec, _const_spec(g_col.shape),
                  pl.BlockSpec((1, t, hd2), lambda b, h, i: (b, i, h)),
                  pl.BlockSpec((1, seq, hd2), lambda b, h, i: (b, 0, DA_HEADS + h)),
                  pl.BlockSpec((1, 1, seq // t, hd2, t), lambda b, h, i: (b, h, 0, 0, 0))],
        out_specs=pl.BlockSpec((1, t, hd2), lambda b, h, i: (b, i, h)),
        out_shape=jax.ShapeDtypeStruct((batch, seq, DA_WIDTH), BF16),
        scratch_shapes=[pltpu.VMEM((2, t, t), F32), pltpu.VMEM((2, t, t), F32),
                        pltpu.VMEM((2, t, t), BF16), pltpu.VMEM((2, t, t), BF16),
                        pltpu.VMEM((2, hd2, t), F32),
                        pltpu.VMEM((2, 1, t), F32)],
        compiler_params=_params("parallel", "parallel", "parallel"),
        name="diff_attn",
    )(lq1, lk1, lq2, lk2, g_col, da, da, vt)


def _gla_kernel(g_ref, main_ref, kend_ref, dec_ref, o_ref, st_ref, stb_ref, of_ref):
    @pl.when(pl.program_id(1) == 0)
    def _():
        st_ref[...] = jnp.zeros_like(st_ref)
        stb_ref[...] = jnp.zeros_like(stb_ref)

    tm = main_ref.shape[0]
    for c in range(tm // CHUNK):
        rows = slice(c * CHUNK, (c + 1) * CHUNK)
        q = main_ref[rows, :GLA_K_WIDTH] * (GLA_DK ** -0.5)
        v = main_ref[rows, 2 * GLA_K_WIDTH:2 * GLA_K_WIDTH + GLA_V_WIDTH]
        upd = lax.dot_general(v, kend_ref[rows, :], _TN, preferred_element_type=F32)
        dec = dec_ref[c:c + 1, :]
        for h in range(GLA_HEADS):
            vr = slice(h * GLA_DV, (h + 1) * GLA_DV)
            kc = slice(h * GLA_DK, (h + 1) * GLA_DK)
            new = st_ref[vr, kc] * dec[:, kc] + upd[vr, kc]
            st_ref[vr, kc] = new
            stb_ref[vr, kc] = new.astype(BF16)
        of_ref[rows, :] = lax.dot_general(q.astype(BF16), stb_ref[...], _NT,
                                          preferred_element_type=F32)

    r = main_ref[:, 2 * GLA_K_WIDTH + GLA_V_WIDTH:].astype(F32)
    gate = r * jax.nn.sigmoid(r)
    for h in range(GLA_HEADS):
        vr = slice(h * GLA_DV, (h + 1) * GLA_DV)
        o = of_ref[:, vr]
        ms = jnp.mean(o * o, axis=-1, keepdims=True)
        o_ref[:, vr] = (o * lax.rsqrt(ms + EPS) * g_ref[...] * gate[:, vr]).astype(BF16)


def _gla(main, kend, dec, g_row, batch, seq):
    n = main.shape[0]
    tm = TOKEN_TILE
    per_b = seq // tm
    idx = lambda b, i: (b * per_b + i, 0)
    return pl.pallas_call(
        _gla_kernel,
        grid=(batch, per_b),
        in_specs=[_const_spec(g_row.shape),
                  pl.BlockSpec((tm, _GLA_MAIN), idx),
                  pl.BlockSpec((tm, GLA_K_WIDTH), idx),
                  pl.BlockSpec((tm // CHUNK, GLA_K_WIDTH), idx)],
        out_specs=pl.BlockSpec((tm, GLA_V_WIDTH), idx),
        out_shape=jax.ShapeDtypeStruct((n, GLA_V_WIDTH), BF16),
        scratch_shapes=[pltpu.VMEM((GLA_V_WIDTH, GLA_K_WIDTH), F32),
                        pltpu.VMEM((GLA_V_WIDTH, GLA_K_WIDTH), BF16),
                        pltpu.VMEM((tm, GLA_V_WIDTH), F32)],
        compiler_params=_params("parallel", "arbitrary"),
        name="gla",
    )(g_row, main, kend, dec)


def _mem_kv_kernel(m_ref, w_ref, k_ref, v_ref):
    y = _dot(m_ref[0].astype(BF16), w_ref[...])
    k_ref[0] = y[:, :D_MODEL].astype(BF16)
    v_ref[0] = y[:, D_MODEL:].astype(BF16)


def _mem_kv(mem, wkv):
    batch, m, _ = mem.shape
    blk = pl.BlockSpec((1, m, D_MODEL), lambda b: (b, 0, 0))
    out = jax.ShapeDtypeStruct((batch, m, D_MODEL), BF16)
    return pl.pallas_call(
        _mem_kv_kernel,
        grid=(batch,),
        in_specs=[blk, _const_spec(wkv.shape)],
        out_specs=[blk, blk],
        out_shape=[out, out],
        compiler_params=_params("parallel"),
        name="mem_kv",
    )(mem, wkv)


def _mix_cross_kernel(oda_ref, og_ref, h_ref, wo1_ref, wo2_ref, g2_ref, b2_ref,
                      wq_ref, k_ref, v_ref, wo_ref, g3_ref, b3_ref, o_ref):
    mix = _dot(oda_ref[...], wo1_ref[...]) + _dot(og_ref[...], wo2_ref[...])
    h2 = _layer_norm(ALPHA * h_ref[...] + mix, g2_ref[...], b2_ref[...])

    q = (_dot(h2.astype(BF16), wq_ref[...]) * (CROSS_HEAD_DIM ** -0.5)).astype(BF16)
    heads = []
    for h in range(CROSS_HEADS):
        cols = slice(h * CROSS_HEAD_DIM, (h + 1) * CROSS_HEAD_DIM)
        s = lax.dot_general(q[:, cols], k_ref[0, :, cols], _NT, preferred_element_type=F32)
        p = jnp.exp(s - jnp.max(s, axis=-1, keepdims=True))
        l = jnp.sum(p, axis=-1, keepdims=True)
        heads.append((_dot(p.astype(BF16), v_ref[0, :, cols]) / l).astype(BF16))
    c = _dot(jnp.concatenate(heads, axis=-1), wo_ref[...])
    o_ref[...] = _layer_norm(ALPHA * h2 + c, g3_ref[...], b3_ref[...])


def _mix_cross(o_da, o_g, h1, wo1, wo2, g2, b2, wq, k_mem, v_mem, wo, g3, b3, seq):
    n = h1.shape[0]
    tm = TOKEN_TILE
    per_b = seq // tm
    row = lambda w: pl.BlockSpec((tm, w), lambda i: (i, 0))
    kv = pl.BlockSpec((1,) + k_mem.shape[1:], lambda i: (i // per_b, 0, 0))
    return pl.pallas_call(
        _mix_cross_kernel,
        grid=(n // tm,),
        in_specs=[row(DA_WIDTH), row(GLA_V_WIDTH), row(D_MODEL),
                  _const_spec(wo1.shape), _const_spec(wo2.shape),
                  _const_spec(g2.shape), _const_spec(b2.shape),
                  _const_spec(wq.shape), kv, kv, _const_spec(wo.shape),
                  _const_spec(g3.shape), _const_spec(b3.shape)],
        out_specs=row(D_MODEL),
        out_shape=jax.ShapeDtypeStruct((n, D_MODEL), F32),
        compiler_params=_params("parallel"),
        name="mix_cross",
    )(o_da, o_g, h1, wo1, wo2, g2, b2, wq, k_mem, v_mem, wo, g3, b3)


def kernel(x, mem, ffn1_w_gate, ffn1_w_up, ffn1_w_down, ln1_g, ln1_b, w_in, da_lambda_q1, da_lambda_k1, da_lambda_q2, da_lambda_k2, da_norm_g, gla_w_gate2, gla_b_gate, gla_norm_g, w_out, ln2_g, ln2_b, cross_wq, cross_wkv, cross_wo, ln3_g, ln3_b, ffn2_w_gate, ffn2_w_up, ffn2_w_down, ln4_g, ln4_b):
    batch, seq, d = x.shape
    assert d == D_MODEL and seq % TOKEN_TILE == 0 and TOKEN_TILE % DA_T == 0
    n = batch * seq
    h = x.reshape(n, d)
    row = lambda a, l: a[l].reshape(1, -1)
    for l in range(DEPTH):
        bf = lambda w: w[l].astype(BF16)
        lambda_init = 0.8 - 0.6 * math.exp(-0.3 * l)
        w_in_l = w_in[l]
        pad = LANES - GLA_GATE_RANK
        w_lr = jnp.pad(w_in_l[:, _OFF_GATE:], ((0, 0), (0, pad))).astype(BF16)
        w_g2 = jnp.pad(gla_w_gate2[l], ((0, pad), (0, 0))).astype(BF16)

        h1 = _ffn_ln(h, bf(ffn1_w_gate), bf(ffn1_w_up), bf(ffn1_w_down), row(ln1_g, l), row(ln1_b, l))
        da, vt, gla_main, kend, dec = _in_proj(
            h1, w_in_l[:, _OFF_DA:_OFF_GLA].astype(BF16), w_in_l[:, _OFF_GLA:_OFF_GATE].astype(BF16),
            w_lr, w_g2, row(gla_b_gate, l), batch, seq)
        o_da = _diff_attn(da.reshape(batch, seq, 2 * DA_WIDTH), vt,
                          row(da_lambda_q1, l), row(da_lambda_k1, l),
                          row(da_lambda_q2, l), row(da_lambda_k2, l),
                          da_norm_g[l].reshape(-1, 1), lambda_init)
        o_g = _gla(gla_main, kend, dec, row(gla_norm_g, l), batch, seq)
        k_mem, v_mem = _mem_kv(mem, bf(cross_wkv))
        w_out_l = bf(w_out)
        h3 = _mix_cross(o_da.reshape(n, DA_WIDTH), o_g, h1, w_out_l[:DA_WIDTH], w_out_l[DA_WIDTH:],
                        row(ln2_g, l), row(ln2_b, l), bf(cross_wq), k_mem, v_mem, bf(cross_wo),
                        row(ln3_g, l), row(ln3_b, l), seq)
        h = _ffn_ln(h3, bf(ffn2_w_gate), bf(ffn2_w_up), bf(ffn2_w_down), row(ln4_g, l), row(ln4_b, l))
    return h.reshape(batch, seq, d)
```

```python
import functools
import math

import jax
import jax.numpy as jnp
from jax import lax
from jax.experimental import pallas as pl
from jax.experimental.pallas import tpu as pltpu

D_MODEL = 1024
CHUNK = 64
DA_HEADS = 4
DA_HEAD_DIM = 64
DA_WIDTH = DA_HEADS * 2 * DA_HEAD_DIM
GLA_HEADS = 4
GLA_DK = 64
GLA_DV = 128
GLA_K_WIDTH = GLA_HEADS * GLA_DK
GLA_V_WIDTH = GLA_HEADS * GLA_DV
GLA_GATE_RANK = 16
GLA_GATE_NORM = 16.0
FFN_HIDDEN = 2816
CROSS_HEADS = 4
CROSS_HEAD_DIM = D_MODEL // CROSS_HEADS
DEPTH = 1
ALPHA = (2.0 * DEPTH) ** 0.25
EPS = 1e-5
LANES = 128

_OFF_DA = 0
_OFF_GLA = 3 * DA_WIDTH
_GLA_MAIN = 2 * GLA_K_WIDTH + 2 * GLA_V_WIDTH
_OFF_GATE = _OFF_GLA + _GLA_MAIN

TOKEN_TILE = 512
FFN_CHUNKS = 2
DA_T = 256
VMEM_LIMIT = 56 * 1024 * 1024

F32 = jnp.float32
BF16 = jnp.bfloat16
_NEG = -1e30
_DA_Q_SCALE = DA_HEAD_DIM ** -0.5 * math.log2(math.e)
_DA_V_PAD = 16
_DA_V_ROWS = 2 * DA_HEAD_DIM + _DA_V_PAD
_DA_UNROLL = 4

_NT = (((1,), (1,)), ((), ()))
_TN = (((0,), (0,)), ((), ()))


def _dot(a, b):
    return jnp.dot(a, b, preferred_element_type=F32)


def _split_bf16(a):
    hi = a.astype(BF16)
    lo = (a - hi.astype(F32)).astype(BF16)
    return hi, lo


def _layer_norm(z, g, b):
    mu = jnp.mean(z, axis=-1, keepdims=True)
    zc = z - mu
    var = jnp.mean(zc * zc, axis=-1, keepdims=True)
    return zc * lax.rsqrt(var + EPS) * g + b


def _const_spec(shape):
    nd = len(shape)
    return pl.BlockSpec(shape, lambda *_: (0,) * nd, pipeline_mode=pl.Buffered(1))


def _params(*sem):
    return pltpu.CompilerParams(dimension_semantics=sem, vmem_limit_bytes=VMEM_LIMIT)


def _ffn_ln_kernel(x_ref, wg_ref, wu_ref, wd_ref, g_ref, b_ref, o_ref):
    x = x_ref[...]
    xb = x.astype(BF16)
    fc = FFN_HIDDEN // FFN_CHUNKS
    y = None
    for c in range(FFN_CHUNKS):
        cols = slice(c * fc, (c + 1) * fc)
        gate = _dot(xb, wg_ref[:, cols])
        up = _dot(xb, wu_ref[:, cols])
        hid = (gate * jax.nn.sigmoid(gate) * up).astype(BF16)
        part = _dot(hid, wd_ref[cols, :])
        y = part if y is None else y + part
    o_ref[...] = _layer_norm(ALPHA * x + 0.5 * y, g_ref[...], b_ref[...])


def _ffn_ln(x, wg, wu, wd, g, b):
    n = x.shape[0]
    tm = TOKEN_TILE
    row = pl.BlockSpec((tm, D_MODEL), lambda i: (i, 0))
    return pl.pallas_call(
        _ffn_ln_kernel,
        grid=(n // tm,),
        in_specs=[row, _const_spec(wg.shape), _const_spec(wu.shape), _const_spec(wd.shape),
                  _const_spec(g.shape), _const_spec(b.shape)],
        out_specs=row,
        out_shape=jax.ShapeDtypeStruct((n, D_MODEL), F32),
        compiler_params=_params("parallel"),
        name="ffn_ln",
    )(x, wg, wu, wd, g, b)


def _in_proj_kernel(h_ref, wda_ref, wgla_ref, wlr_ref, wg2_ref, bg_ref,
                    qt_ref, k_ref, vt_ref, gla_ref, kend_ref, dec_ref):
    tm = h_ref.shape[0]
    hb = h_ref[...].astype(BF16)
    hd2 = 2 * DA_HEAD_DIM

    y_da = _dot(hb, wda_ref[...])
    k_ref[...] = y_da[:, DA_WIDTH:2 * DA_WIDTH].astype(BF16)
    q_t = (y_da[:, :DA_WIDTH] * _DA_Q_SCALE).T
    v_t = y_da[:, 2 * DA_WIDTH:].T
    sub = lax.broadcasted_iota(jnp.int32, (_DA_V_PAD, DA_T), 0)
    ones_row = jnp.where(sub == 0, 1.0, 0.0).astype(BF16)
    for h in range(DA_HEADS):
        for j in range(tm // DA_T):
            rows, cols = slice(h * hd2, (h + 1) * hd2), slice(j * DA_T, (j + 1) * DA_T)
            qt_ref[0, h, j] = q_t[rows, cols].astype(BF16)
            vt_ref[0, h, j, :hd2, :] = v_t[rows, cols].astype(BF16)
            vt_ref[0, h, j, hd2:, :] = ones_row

    y_g = _dot(hb, wgla_ref[...])
    gla_ref[...] = y_g.astype(BF16)

    g_lr = _dot(hb, wlr_ref[...])
    pre = _dot(g_lr.astype(BF16), wg2_ref[...]) + bg_ref[...]
    log_a = jax.nn.log_sigmoid(pre) / GLA_GATE_NORM

    r = lax.broadcasted_iota(jnp.int32, (tm, tm), 0)
    c = lax.broadcasted_iota(jnp.int32, (tm, tm), 1)
    later = jnp.where((c > r) & ((c // CHUNK) == (r // CHUNK)), 1.0, 0.0).astype(BF16)
    la_hi, la_lo = _split_bf16(log_a)
    rev = _dot(later, la_hi) + _dot(later, la_lo)
    k_g = y_g[:, GLA_K_WIDTH:2 * GLA_K_WIDTH]
    kend_ref[...] = (k_g * jnp.exp(rev)).astype(BF16)
    total = jnp.sum(log_a.reshape(tm // CHUNK, CHUNK, GLA_K_WIDTH), axis=1)
    dec_ref[...] = jnp.exp(total)


def _in_proj(h, w_da, w_gla, w_lr, w_g2, b_g, batch, seq):
    n = h.shape[0]
    tm = TOKEN_TILE
    per_b = seq // tm
    row = lambda w: pl.BlockSpec((tm, w), lambda i: (i, 0))
    blocked = lambda i: (i // per_b, 0, i % per_b, 0, 0)
    return pl.pallas_call(
        _in_proj_kernel,
        grid=(n // tm,),
        in_specs=[row(D_MODEL), _const_spec(w_da.shape), _const_spec(w_gla.shape),
                  _const_spec(w_lr.shape), _const_spec(w_g2.shape), _const_spec(b_g.shape)],
        out_specs=[
            pl.BlockSpec((1, DA_HEADS, tm // DA_T, 2 * DA_HEAD_DIM, DA_T), blocked),
            row(DA_WIDTH),
            pl.BlockSpec((1, DA_HEADS, tm // DA_T, _DA_V_ROWS, DA_T), blocked),
            row(_GLA_MAIN),
            row(GLA_K_WIDTH),
            pl.BlockSpec((tm // CHUNK, GLA_K_WIDTH), lambda i: (i, 0)),
        ],
        out_shape=[
            jax.ShapeDtypeStruct((batch, DA_HEADS, seq // DA_T, 2 * DA_HEAD_DIM, DA_T), BF16),
            jax.ShapeDtypeStruct((n, DA_WIDTH), BF16),
            jax.ShapeDtypeStruct((batch, DA_HEADS, seq // DA_T, _DA_V_ROWS, DA_T), BF16),
            jax.ShapeDtypeStruct((n, _GLA_MAIN), BF16),
            jax.ShapeDtypeStruct((n, GLA_K_WIDTH), BF16),
            jax.ShapeDtypeStruct((n // CHUNK, GLA_K_WIDTH), F32),
        ],
        compiler_params=_params("parallel"),
        name="in_proj",
    )(h, w_da, w_gla, w_lr, w_g2, b_g)


def _diff_attn_kernel(it_ref, jt_ref, lq1_ref, lk1_ref, lq2_ref, lk2_ref, g_ref, e_ref, mk_ref,
                      qt_ref, k_ref, vt_ref, o_ref, s_a, s_b, p_a, p_b, acc_ref,
                      *, lambda_init, n_items):
    t = DA_T
    hd2 = 2 * DA_HEAD_DIM
    s_bufs = (s_a, s_b)
    p_bufs = (p_a, p_b)
    row = lax.broadcasted_iota(jnp.int32, (hd2, t), 0)
    halves = (row < DA_HEAD_DIM, row >= DA_HEAD_DIM)

    def scores(x, s_ref):
        i, j = it_ref[x], jt_ref[x]
        kj = k_ref[0, pl.ds(pl.multiple_of(j * t, t), t), :]
        k_ext = jnp.concatenate([kj, e_ref[...]], axis=1)
        qt = qt_ref[0, 0, i]
        diag = jnp.where(i == j, 1.0, 0.0)
        mk = (mk_ref[...] * diag).astype(BF16)
        pad = jnp.zeros((hd2 - mk.shape[0], t), BF16)
        for c in range(2):
            rhs = jnp.concatenate([jnp.where(halves[c], qt, 0), mk, pad], axis=0)
            s_ref[c] = _dot(k_ext, rhs)

    def softmax(x, s_ref, p_ref, m):
        first = jt_ref[x] == 0
        m_out, a_out = [], []
        for c in range(2):
            s = s_ref[c]
            m_old = jnp.where(first, _NEG, m[c])
            m_new = jnp.maximum(m_old, jnp.max(s, axis=0, keepdims=True))
            p_ref[c] = jnp.exp2(s - m_new).astype(BF16)
            m_out.append(m_new)
            a_out.append(jnp.exp2(m_old - m_new))
        return tuple(m_out), tuple(a_out)

    def values(x, p_ref, a):
        i, j = it_ref[x], jt_ref[x]
        vtj = vt_ref[0, 0, j]
        for c in range(2):
            acc_ref[i, c] = a[c] * acc_ref[i, c] + _dot(vtj, p_ref[c])

    acc_ref[...] = jnp.zeros_like(acc_ref)
    p_b[...] = jnp.zeros_like(p_b)
    scores(0, s_a)

    def body(step, carry):
        m, a_prev = carry
        for u in range(_DA_UNROLL):
            x = step * _DA_UNROLL + u
            scores(x + 1, s_bufs[(u + 1) % 2])
            m, a = softmax(x, s_bufs[u % 2], p_bufs[u % 2], m)
            values(jnp.maximum(x - 1, 0), p_bufs[(u + 1) % 2], a_prev)
            a_prev = a
        return m, a_prev

    neg = jnp.full((1, t), _NEG, F32)
    one = jnp.ones((1, t), F32)
    _, a_last = lax.fori_loop(0, n_items // _DA_UNROLL, body, ((neg, neg), (one, one)))
    values(n_items - 1, p_bufs[(n_items - 1) % 2], a_last)

    lam = (jnp.exp(jnp.sum(lq1_ref[...] * lk1_ref[...], axis=-1, keepdims=True))
           - jnp.exp(jnp.sum(lq2_ref[...] * lk2_ref[...], axis=-1, keepdims=True))
           + lambda_init)

    def finish(i, _):
        a0, a1 = acc_ref[i, 0], acc_ref[i, 1]
        o = a0[:hd2] / a0[hd2:hd2 + 1] - lam * (a1[:hd2] / a1[hd2:hd2 + 1])
        ms = jnp.mean(o * o, axis=0, keepdims=True)
        y = o * lax.rsqrt(ms + EPS) * g_ref[...] * (1.0 - lambda_init)
        o_ref[0, pl.ds(pl.multiple_of(i * t, t), t), :] = y.T.astype(BF16)
        return 0

    lax.fori_loop(0, acc_ref.shape[0], finish, 0)


def _diff_attn(qt, k, vt, lq1, lk1, lq2, lk2, g_col, lambda_init):
    batch, seq, _ = k.shape
    hd2 = 2 * DA_HEAD_DIM
    t = DA_T
    nq = seq // t
    items = [(i, j) for i in range(nq) for j in range(i + 1)]
    n_items = len(items)
    assert n_items % _DA_UNROLL == 0 and _DA_UNROLL % 2 == 0 and t % CHUNK == 0
    items.append(items[-1])
    i_tab = jnp.asarray([i for i, _ in items], jnp.int32)
    j_tab = jnp.asarray([j for _, j in items], jnp.int32)
    chunk = jnp.arange(t) // CHUNK
    e = (chunk[:, None] == jnp.arange(hd2)[None, :]).astype(BF16)
    ids = jnp.arange(_DA_V_PAD)
    mk = jnp.where((ids[:, None] < t // CHUNK) & (chunk[None, :] < ids[:, None]), _NEG, 0.0).astype(F32)

    const = lambda shape: pl.BlockSpec(shape, lambda *_: (0,) * len(shape), pipeline_mode=pl.Buffered(1))
    head = lambda rows: pl.BlockSpec((1, 1, nq, rows, t), lambda b, h, *_: (b, h, 0, 0, 0))
    return pl.pallas_call(
        functools.partial(_diff_attn_kernel, lambda_init=lambda_init, n_items=n_items),
        grid_spec=pltpu.PrefetchScalarGridSpec(
            num_scalar_prefetch=2,
            grid=(batch, DA_HEADS),
            in_specs=[const(lq1.shape), const(lk1.shape), const(lq2.shape), const(lk2.shape),
                      const(g_col.shape), const(e.shape), const(mk.shape),
                      head(hd2),
                      pl.BlockSpec((1, seq, hd2), lambda b, h, *_: (b, 0, h)),
                      head(_DA_V_ROWS)],
            out_specs=pl.BlockSpec((1, seq, hd2), lambda b, h, *_: (b, 0, h)),
            scratch_shapes=[pltpu.VMEM((2, t, t), F32), pltpu.VMEM((2, t, t), F32),
                            pltpu.VMEM((2, t, t), BF16), pltpu.VMEM((2, t, t), BF16),
                            pltpu.VMEM((nq, 2, _DA_V_ROWS, t), F32)],
        ),
        out_shape=jax.ShapeDtypeStruct((batch, seq, DA_WIDTH), BF16),
        compiler_params=_params("parallel", "parallel"),
        name="diff_attn",
    )(i_tab, j_tab, lq1, lk1, lq2, lk2, g_col, e, mk, qt, k, vt)


def _gla_kernel(g_ref, main_ref, kend_ref, dec_ref, o_ref, st_ref, stb_ref, of_ref):
    @pl.when(pl.program_id(1) == 0)
    def _():
        st_ref[...] = jnp.zeros_like(st_ref)
        stb_ref[...] = jnp.zeros_like(stb_ref)

    tm = main_ref.shape[0]
    for c in range(tm // CHUNK):
        rows = slice(c * CHUNK, (c + 1) * CHUNK)
        q = main_ref[rows, :GLA_K_WIDTH] * (GLA_DK ** -0.5)
        v = main_ref[rows, 2 * GLA_K_WIDTH:2 * GLA_K_WIDTH + GLA_V_WIDTH]
        upd = lax.dot_general(v, kend_ref[rows, :], _TN, preferred_element_type=F32)
        dec = dec_ref[c:c + 1, :]
        for h in range(GLA_HEADS):
            vr = slice(h * GLA_DV, (h + 1) * GLA_DV)
            kc = slice(h * GLA_DK, (h + 1) * GLA_DK)
            new = st_ref[vr, kc] * dec[:, kc] + upd[vr, kc]
            st_ref[vr, kc] = new
            stb_ref[vr, kc] = new.astype(BF16)
        of_ref[rows, :] = lax.dot_general(q.astype(BF16), stb_ref[...], _NT,
                                          preferred_element_type=F32)

    r = main_ref[:, 2 * GLA_K_WIDTH + GLA_V_WIDTH:].astype(F32)
    gate = r * jax.nn.sigmoid(r)
    for h in range(GLA_HEADS):
        vr = slice(h * GLA_DV, (h + 1) * GLA_DV)
        o = of_ref[:, vr]
        ms = jnp.mean(o * o, axis=-1, keepdims=True)
        o_ref[:, vr] = (o * lax.rsqrt(ms + EPS) * g_ref[...] * gate[:, vr]).astype(BF16)


def _gla(main, kend, dec, g_row, batch, seq):
    n = main.shape[0]
    tm = TOKEN_TILE
    per_b = seq // tm
    idx = lambda b, i: (b * per_b + i, 0)
    return pl.pallas_call(
        _gla_kernel,
        grid=(batch, per_b),
        in_specs=[_const_spec(g_row.shape),
                  pl.BlockSpec((tm, _GLA_MAIN), idx),
                  pl.BlockSpec((tm, GLA_K_WIDTH), idx),
                  pl.BlockSpec((tm // CHUNK, GLA_K_WIDTH), idx)],
        out_specs=pl.BlockSpec((tm, GLA_V_WIDTH), idx),
        out_shape=jax.ShapeDtypeStruct((n, GLA_V_WIDTH), BF16),
        scratch_shapes=[pltpu.VMEM((GLA_V_WIDTH, GLA_K_WIDTH), F32),
                        pltpu.VMEM((GLA_V_WIDTH, GLA_K_WIDTH), BF16),
                        pltpu.VMEM((tm, GLA_V_WIDTH), F32)],
        compiler_params=_params("parallel", "arbitrary"),
        name="gla",
    )(g_row, main, kend, dec)


def _mem_kv_kernel(m_ref, w_ref, k_ref, v_ref):
    y = _dot(m_ref[0].astype(BF16), w_ref[...])
    k_ref[0] = y[:, :D_MODEL].astype(BF16)
    v_ref[0] = y[:, D_MODEL:].astype(BF16)


def _mem_kv(mem, wkv):
    batch, m, _ = mem.shape
    blk = pl.BlockSpec((1, m, D_MODEL), lambda b: (b, 0, 0))
    out = jax.ShapeDtypeStruct((batch, m, D_MODEL), BF16)
    return pl.pallas_call(
        _mem_kv_kernel,
        grid=(batch,),
        in_specs=[blk, _const_spec(wkv.shape)],
        out_specs=[blk, blk],
        out_shape=[out, out],
        compiler_params=_params("parallel"),
        name="mem_kv",
    )(mem, wkv)


def _mix_cross_kernel(oda_ref, og_ref, h_ref, wo1_ref, wo2_ref, g2_ref, b2_ref,
                      wq_ref, k_ref, v_ref, wo_ref, g3_ref, b3_ref, o_ref):
    mix = _dot(oda_ref[...], wo1_ref[...]) + _dot(og_ref[...], wo2_ref[...])
    h2 = _layer_norm(ALPHA * h_ref[...] + mix, g2_ref[...], b2_ref[...])

    q = (_dot(h2.astype(BF16), wq_ref[...]) * (CROSS_HEAD_DIM ** -0.5)).astype(BF16)
    heads = []
    for h in range(CROSS_HEADS):
        cols = slice(h * CROSS_HEAD_DIM, (h + 1) * CROSS_HEAD_DIM)
        s = lax.dot_general(q[:, cols], k_ref[0, :, cols], _NT, preferred_element_type=F32)
        p = jnp.exp(s - jnp.max(s, axis=-1, keepdims=True))
        l = jnp.sum(p, axis=-1, keepdims=True)
        heads.append((_dot(p.astype(BF16), v_ref[0, :, cols]) / l).astype(BF16))
    c = _dot(jnp.concatenate(heads, axis=-1), wo_ref[...])
    o_ref[...] = _layer_norm(ALPHA * h2 + c, g3_ref[...], b3_ref[...])


def _mix_cross(o_da, o_g, h1, wo1, wo2, g2, b2, wq, k_mem, v_mem, wo, g3, b3, seq):
    n = h1.shape[0]
    tm = TOKEN_TILE
    per_b = seq // tm
    row = lambda w: pl.BlockSpec((tm, w), lambda i: (i, 0))
    kv = pl.BlockSpec((1,) + k_mem.shape[1:], lambda i: (i // per_b, 0, 0))
    return pl.pallas_call(
        _mix_cross_kernel,
        grid=(n // tm,),
        in_specs=[row(DA_WIDTH), row(GLA_V_WIDTH), row(D_MODEL),
                  _const_spec(wo1.shape), _const_spec(wo2.shape),
                  _const_spec(g2.shape), _const_spec(b2.shape),
                  _const_spec(wq.shape), kv, kv, _const_spec(wo.shape),
                  _const_spec(g3.shape), _const_spec(b3.shape)],
        out_specs=row(D_MODEL),
        out_shape=jax.ShapeDtypeStruct((n, D_MODEL), F32),
        compiler_params=_params("parallel"),
        name="mix_cross",
    )(o_da, o_g, h1, wo1, wo2, g2, b2, wq, k_mem, v_mem, wo, g3, b3)


def kernel(x, mem, ffn1_w_gate, ffn1_w_up, ffn1_w_down, ln1_g, ln1_b, w_in, da_lambda_q1, da_lambda_k1, da_lambda_q2, da_lambda_k2, da_norm_g, gla_w_gate2, gla_b_gate, gla_norm_g, w_out, ln2_g, ln2_b, cross_wq, cross_wkv, cross_wo, ln3_g, ln3_b, ffn2_w_gate, ffn2_w_up, ffn2_w_down, ln4_g, ln4_b):
    batch, seq, d = x.shape
    assert d == D_MODEL and seq % TOKEN_TILE == 0 and TOKEN_TILE % DA_T == 0
    n = batch * seq
    h = x.reshape(n, d)
    row = lambda a, l: a[l].reshape(1, -1)
    for l in range(DEPTH):
        bf = lambda w: w[l].astype(BF16)
        lambda_init = 0.8 - 0.6 * math.exp(-0.3 * l)
        w_in_l = w_in[l]
        pad = LANES - GLA_GATE_RANK
        w_lr = jnp.pad(w_in_l[:, _OFF_GATE:], ((0, 0), (0, pad))).astype(BF16)
        w_g2 = jnp.pad(gla_w_gate2[l], ((0, pad), (0, 0))).astype(BF16)

        h1 = _ffn_ln(h, bf(ffn1_w_gate), bf(ffn1_w_up), bf(ffn1_w_down), row(ln1_g, l), row(ln1_b, l))
        qt, k_da, vt, gla_main, kend, dec = _in_proj(
            h1, w_in_l[:, _OFF_DA:_OFF_GLA].astype(BF16), w_in_l[:, _OFF_GLA:_OFF_GATE].astype(BF16),
            w_lr, w_g2, row(gla_b_gate, l), batch, seq)
        o_da = _diff_attn(qt, k_da.reshape(batch, seq, DA_WIDTH), vt,
                          row(da_lambda_q1, l), row(da_lambda_k1, l),
                          row(da_lambda_q2, l), row(da_lambda_k2, l),
                          da_norm_g[l].reshape(-1, 1), lambda_init)
        o_g = _gla(gla_main, kend, dec, row(gla_norm_g, l), batch, seq)
        k_mem, v_mem = _mem_kv(mem, bf(cross_wkv))
        w_out_l = bf(w_out)
        h3 = _mix_cross(o_da.reshape(n, DA_WIDTH), o_g, h1, w_out_l[:DA_WIDTH], w_out_l[DA_WIDTH:],
                        row(ln2_g, l), row(ln2_b, l), bf(cross_wq), k_mem, v_mem, bf(cross_wo),
                        row(ln3_g, l), row(ln3_b, l), seq)
        h = _ffn_ln(h3, bf(ffn2_w_gate), bf(ffn2_w_up), bf(ffn2_w_down), row(ln4_g, l), row(ln4_b, l))
    return h.reshape(batch, seq, d)
```

```python
import functools
import math

import jax
import jax.numpy as jnp
from jax import lax
from jax.experimental import pallas as pl
from jax.experimental.pallas import tpu as pltpu

D_MODEL = 1024
CHUNK = 64
DA_HEADS = 4
DA_HEAD_DIM = 64
DA_WIDTH = DA_HEADS * 2 * DA_HEAD_DIM
GLA_HEADS = 4
GLA_DK = 64
GLA_DV = 128
GLA_K_WIDTH = GLA_HEADS * GLA_DK
GLA_V_WIDTH = GLA_HEADS * GLA_DV
GLA_GATE_RANK = 16
GLA_GATE_NORM = 16.0
FFN_HIDDEN = 2816
CROSS_HEADS = 4
CROSS_HEAD_DIM = D_MODEL // CROSS_HEADS
DEPTH = 1
ALPHA = (2.0 * DEPTH) ** 0.25
EPS = 1e-5
LANES = 128

_OFF_DA = 0
_OFF_GLA = 3 * DA_WIDTH
_GLA_MAIN = 2 * GLA_K_WIDTH + 2 * GLA_V_WIDTH
_OFF_GATE = _OFF_GLA + _GLA_MAIN

TOKEN_TILE = 512
MXU_DIM = 256
FFN_SPLITS = (0, 1536, FFN_HIDDEN)
assert all(s % MXU_DIM == 0 for s in FFN_SPLITS)
MIX_ROW_GROUPS = 2
DA_T = 256
VMEM_LIMIT = 56 * 1024 * 1024

F32 = jnp.float32
BF16 = jnp.bfloat16
_NEG = -1e30
_DA_Q_SCALE = DA_HEAD_DIM ** -0.5 * math.log2(math.e)
_DA_V_PAD = 16
_DA_V_ROWS = 2 * DA_HEAD_DIM + _DA_V_PAD
_DA_UNROLL = 8

_NT = (((1,), (1,)), ((), ()))
_TN = (((0,), (0,)), ((), ()))


def _dot(a, b):
    return jnp.dot(a, b, preferred_element_type=F32)


def _split_bf16(a):
    hi = a.astype(BF16)
    lo = (a - hi.astype(F32)).astype(BF16)
    return hi, lo


def _layer_norm(z, g, b):
    mu = jnp.mean(z, axis=-1, keepdims=True)
    zc = z - mu
    var = jnp.mean(zc * zc, axis=-1, keepdims=True)
    return zc * lax.rsqrt(var + EPS) * g + b


def _const_spec(shape):
    nd = len(shape)
    return pl.BlockSpec(shape, lambda *_: (0,) * nd, pipeline_mode=pl.Buffered(1))


def _params(*sem):
    return pltpu.CompilerParams(dimension_semantics=sem, vmem_limit_bytes=VMEM_LIMIT)


def _ffn_ln_kernel(x_ref, wg_ref, wu_ref, wd_ref, g_ref, b_ref, o_ref):
    x = x_ref[...]
    xb = x.astype(BF16)
    y = None
    for lo, hi in zip(FFN_SPLITS[:-1], FFN_SPLITS[1:]):
        cols = slice(lo, hi)
        gate = _dot(xb, wg_ref[:, cols])
        up = _dot(xb, wu_ref[:, cols])
        hid = (gate * jax.nn.sigmoid(gate) * up).astype(BF16)
        part = _dot(hid, wd_ref[cols, :])
        y = part if y is None else y + part
    o_ref[...] = _layer_norm(ALPHA * x + 0.5 * y, g_ref[...], b_ref[...])


def _ffn_ln(x, wg, wu, wd, g, b):
    n = x.shape[0]
    tm = TOKEN_TILE
    row = pl.BlockSpec((tm, D_MODEL), lambda i: (i, 0))
    return pl.pallas_call(
        _ffn_ln_kernel,
        grid=(n // tm,),
        in_specs=[row, _const_spec(wg.shape), _const_spec(wu.shape), _const_spec(wd.shape),
                  _const_spec(g.shape), _const_spec(b.shape)],
        out_specs=row,
        out_shape=jax.ShapeDtypeStruct((n, D_MODEL), F32),
        compiler_params=_params("parallel"),
        name="ffn_ln",
    )(x, wg, wu, wd, g, b)


def _in_proj_kernel(h_ref, wda_ref, wgla_ref, wlr_ref, wg2_ref, bg_ref, later_ref,
                    qt_ref, k_ref, vt_ref, gla_ref, kend_ref, dec_ref):
    tm = h_ref.shape[0]
    hb = h_ref[...].astype(BF16)
    hd2 = 2 * DA_HEAD_DIM

    g_lr = _dot(hb, wlr_ref[...])
    pre = _dot(g_lr.astype(BF16), wg2_ref[...]) + bg_ref[...]
    log_a = (jnp.minimum(pre, 0.0) - jnp.log(1.0 + jnp.exp(-jnp.abs(pre)))) / GLA_GATE_NORM
    la_hi, la_lo = _split_bf16(log_a)
    rev = _dot(later_ref[...], la_hi) + _dot(later_ref[...], la_lo)
    to_end = jnp.exp(rev)
    total = jnp.sum(log_a.reshape(tm // CHUNK, CHUNK, GLA_K_WIDTH), axis=1)
    dec_ref[...] = jnp.exp(total)

    y_g = _dot(hb, wgla_ref[...])
    gla_ref[...] = y_g.astype(BF16)
    kend_ref[...] = (y_g[:, GLA_K_WIDTH:2 * GLA_K_WIDTH] * to_end).astype(BF16)

    y_da = _dot(hb, wda_ref[...])
    k_ref[...] = y_da[:, DA_WIDTH:2 * DA_WIDTH].astype(BF16)
    q_t = (y_da[:, :DA_WIDTH] * _DA_Q_SCALE).T
    v_t = y_da[:, 2 * DA_WIDTH:].T
    sub = lax.broadcasted_iota(jnp.int32, (_DA_V_PAD, DA_T), 0)
    ones_row = jnp.where(sub == 0, 1.0, 0.0).astype(BF16)
    for h in range(DA_HEADS):
        for j in range(tm // DA_T):
            rows, cols = slice(h * hd2, (h + 1) * hd2), slice(j * DA_T, (j + 1) * DA_T)
            qt_ref[0, h, j] = q_t[rows, cols].astype(BF16)
            vt_ref[0, h, j, :hd2, :] = v_t[rows, cols].astype(BF16)
            vt_ref[0, h, j, hd2:, :] = ones_row


def _in_proj(h, w_da, w_gla, w_lr, w_g2, b_g, batch, seq):
    n = h.shape[0]
    tm = TOKEN_TILE
    per_b = seq // tm
    row = lambda w: pl.BlockSpec((tm, w), lambda i: (i, 0))
    blocked = lambda i: (i // per_b, 0, i % per_b, 0, 0)
    step = jnp.arange(tm)
    later = ((step[None, :] > step[:, None])
             & (step[None, :] // CHUNK == step[:, None] // CHUNK)).astype(BF16)
    return pl.pallas_call(
        _in_proj_kernel,
        grid=(n // tm,),
        in_specs=[row(D_MODEL), _const_spec(w_da.shape), _const_spec(w_gla.shape),
                  _const_spec(w_lr.shape), _const_spec(w_g2.shape), _const_spec(b_g.shape),
                  _const_spec(later.shape)],
        out_specs=[
            pl.BlockSpec((1, DA_HEADS, tm // DA_T, 2 * DA_HEAD_DIM, DA_T), blocked),
            row(DA_WIDTH),
            pl.BlockSpec((1, DA_HEADS, tm // DA_T, _DA_V_ROWS, DA_T), blocked),
            row(_GLA_MAIN),
            row(GLA_K_WIDTH),
            pl.BlockSpec((tm // CHUNK, GLA_K_WIDTH), lambda i: (i, 0)),
        ],
        out_shape=[
            jax.ShapeDtypeStruct((batch, DA_HEADS, seq // DA_T, 2 * DA_HEAD_DIM, DA_T), BF16),
            jax.ShapeDtypeStruct((n, DA_WIDTH), BF16),
            jax.ShapeDtypeStruct((batch, DA_HEADS, seq // DA_T, _DA_V_ROWS, DA_T), BF16),
            jax.ShapeDtypeStruct((n, _GLA_MAIN), BF16),
            jax.ShapeDtypeStruct((n, GLA_K_WIDTH), BF16),
            jax.ShapeDtypeStruct((n // CHUNK, GLA_K_WIDTH), F32),
        ],
        compiler_params=_params("parallel"),
        name="in_proj",
    )(h, w_da, w_gla, w_lr, w_g2, b_g, later)


def _diff_attn_kernel(it_ref, jt_ref, lq1_ref, lk1_ref, lq2_ref, lk2_ref, g_ref, e_ref, mk_ref,
                      qt_ref, k_ref, vt_ref, o_ref, s_a, s_b, p_a, p_b, acc_ref,
                      *, lambda_init, n_items):
    t = DA_T
    hd2 = 2 * DA_HEAD_DIM
    s_bufs = (s_a, s_b)
    p_bufs = (p_a, p_b)
    row = lax.broadcasted_iota(jnp.int32, (hd2, t), 0)
    halves = (row < DA_HEAD_DIM, row >= DA_HEAD_DIM)

    def scores(x, s_ref):
        i, j = it_ref[x], jt_ref[x]
        kj = k_ref[0, pl.ds(pl.multiple_of(j * t, t), t), :]
        k_ext = jnp.concatenate([kj, e_ref[...]], axis=1)
        qt = qt_ref[0, 0, i]
        diag = jnp.where(i == j, 1.0, 0.0)
        mk = (mk_ref[...] * diag).astype(BF16)
        pad = jnp.zeros((hd2 - mk.shape[0], t), BF16)
        for c in range(2):
            rhs = jnp.concatenate([jnp.where(halves[c], qt, 0), mk, pad], axis=0)
            s_ref[c] = _dot(k_ext, rhs)

    def softmax(x, s_ref, p_ref, m):
        first = jt_ref[x] == 0
        m_out, a_out = [], []
        for c in range(2):
            s = s_ref[c]
            m_old = jnp.where(first, _NEG, m[c])
            m_new = jnp.maximum(m_old, jnp.max(s, axis=0, keepdims=True))
            p_ref[c] = jnp.exp2(s - m_new).astype(BF16)
            m_out.append(m_new)
            a_out.append(jnp.exp2(m_old - m_new))
        return tuple(m_out), tuple(a_out)

    def values(x, p_ref, a):
        i, j = it_ref[x], jt_ref[x]
        vtj = vt_ref[0, 0, j]
        for c in range(2):
            acc_ref[i, c] = a[c] * acc_ref[i, c] + _dot(vtj, p_ref[c])

    acc_ref[...] = jnp.zeros_like(acc_ref)
    p_b[...] = jnp.zeros_like(p_b)
    scores(0, s_a)

    def body(step, carry):
        m, a_prev = carry
        for u in range(_DA_UNROLL):
            x = step * _DA_UNROLL + u
            scores(x + 1, s_bufs[(u + 1) % 2])
            m, a = softmax(x, s_bufs[u % 2], p_bufs[u % 2], m)
            values(jnp.maximum(x - 1, 0), p_bufs[(u + 1) % 2], a_prev)
            a_prev = a
        return m, a_prev

    neg = jnp.full((1, t), _NEG, F32)
    one = jnp.ones((1, t), F32)
    _, a_last = lax.fori_loop(0, n_items // _DA_UNROLL, body, ((neg, neg), (one, one)))
    values(n_items - 1, p_bufs[(n_items - 1) % 2], a_last)

    lam = (jnp.exp(jnp.sum(lq1_ref[...] * lk1_ref[...], axis=-1, keepdims=True))
           - jnp.exp(jnp.sum(lq2_ref[...] * lk2_ref[...], axis=-1, keepdims=True))
           + lambda_init)

    def finish(i, _):
        a0, a1 = acc_ref[i, 0], acc_ref[i, 1]
        o = a0[:hd2] / a0[hd2:hd2 + 1] - lam * (a1[:hd2] / a1[hd2:hd2 + 1])
        ms = jnp.mean(o * o, axis=0, keepdims=True)
        y = o * lax.rsqrt(ms + EPS) * g_ref[...] * (1.0 - lambda_init)
        o_ref[0, pl.ds(pl.multiple_of(i * t, t), t), :] = y.T.astype(BF16)
        return 0

    lax.fori_loop(0, acc_ref.shape[0], finish, 0, unroll=2)


def _diff_attn(qt, k, vt, lq1, lk1, lq2, lk2, g_col, lambda_init):
    batch, seq, _ = k.shape
    hd2 = 2 * DA_HEAD_DIM
    t = DA_T
    nq = seq // t
    items = [(i, j) for i in range(nq) for j in range(i + 1)]
    n_items = len(items)
    assert n_items % _DA_UNROLL == 0 and _DA_UNROLL % 2 == 0 and t % CHUNK == 0
    items.append(items[-1])
    i_tab = jnp.asarray([i for i, _ in items], jnp.int32)
    j_tab = jnp.asarray([j for _, j in items], jnp.int32)
    chunk = jnp.arange(t) // CHUNK
    e = (chunk[:, None] == jnp.arange(hd2)[None, :]).astype(BF16)
    ids = jnp.arange(_DA_V_PAD)
    mk = jnp.where((ids[:, None] < t // CHUNK) & (chunk[None, :] < ids[:, None]), _NEG, 0.0).astype(F32)

    const = lambda shape: pl.BlockSpec(shape, lambda *_: (0,) * len(shape), pipeline_mode=pl.Buffered(1))
    head = lambda rows: pl.BlockSpec((1, 1, nq, rows, t), lambda b, h, *_: (b, h, 0, 0, 0))
    return pl.pallas_call(
        functools.partial(_diff_attn_kernel, lambda_init=lambda_init, n_items=n_items),
        grid_spec=pltpu.PrefetchScalarGridSpec(
            num_scalar_prefetch=2,
            grid=(batch, DA_HEADS),
            in_specs=[const(lq1.shape), const(lk1.shape), const(lq2.shape), const(lk2.shape),
                      const(g_col.shape), const(e.shape), const(mk.shape),
                      head(hd2),
                      pl.BlockSpec((1, seq, hd2), lambda b, h, *_: (b, 0, h)),
                      head(_DA_V_ROWS)],
            out_specs=pl.BlockSpec((1, seq, hd2), lambda b, h, *_: (b, 0, h)),
            scratch_shapes=[pltpu.VMEM((2, t, t), F32), pltpu.VMEM((2, t, t), F32),
                            pltpu.VMEM((2, t, t), BF16), pltpu.VMEM((2, t, t), BF16),
                            pltpu.VMEM((nq, 2, _DA_V_ROWS, t), F32)],
        ),
        out_shape=jax.ShapeDtypeStruct((batch, seq, DA_WIDTH), BF16),
        compiler_params=_params("parallel", "parallel"),
        name="diff_attn",
    )(i_tab, j_tab, lq1, lk1, lq2, lk2, g_col, e, mk, qt, k, vt)


def _gla_kernel(g_ref, main_ref, kend_ref, dec_ref, o_ref, st_ref, stb_ref, of_ref):
    @pl.when(pl.program_id(1) == 0)
    def _():
        st_ref[...] = jnp.zeros_like(st_ref)
        stb_ref[...] = jnp.zeros_like(stb_ref)

    tm = main_ref.shape[0]
    for c in range(tm // CHUNK):
        rows = slice(c * CHUNK, (c + 1) * CHUNK)
        q = main_ref[rows, :GLA_K_WIDTH] * (GLA_DK ** -0.5)
        v = main_ref[rows, 2 * GLA_K_WIDTH:2 * GLA_K_WIDTH + GLA_V_WIDTH]
        upd = lax.dot_general(v, kend_ref[rows, :], _TN, preferred_element_type=F32)
        dec = dec_ref[c:c + 1, :]
        for h in range(GLA_HEADS):
            vr = slice(h * GLA_DV, (h + 1) * GLA_DV)
            kc = slice(h * GLA_DK, (h + 1) * GLA_DK)
            new = st_ref[vr, kc] * dec[:, kc] + upd[vr, kc]
            st_ref[vr, kc] = new
            stb_ref[vr, kc] = new.astype(BF16)
        of_ref[rows, :] = lax.dot_general(q.astype(BF16), stb_ref[...], _NT,
                                          preferred_element_type=F32)

    r = main_ref[:, 2 * GLA_K_WIDTH + GLA_V_WIDTH:].astype(F32)
    gate = r * jax.nn.sigmoid(r)
    for h in range(GLA_HEADS):
        vr = slice(h * GLA_DV, (h + 1) * GLA_DV)
        o = of_ref[:, vr]
        ms = jnp.mean(o * o, axis=-1, keepdims=True)
        o_ref[:, vr] = (o * lax.rsqrt(ms + EPS) * g_ref[...] * gate[:, vr]).astype(BF16)


def _gla(main, kend, dec, g_row, batch, seq):
    n = main.shape[0]
    tm = TOKEN_TILE
    per_b = seq // tm
    idx = lambda b, i: (b * per_b + i, 0)
    return pl.pallas_call(
        _gla_kernel,
        grid=(batch, per_b),
        in_specs=[_const_spec(g_row.shape),
                  pl.BlockSpec((tm, _GLA_MAIN), idx),
                  pl.BlockSpec((tm, GLA_K_WIDTH), idx),
                  pl.BlockSpec((tm // CHUNK, GLA_K_WIDTH), idx)],
        out_specs=pl.BlockSpec((tm, GLA_V_WIDTH), idx),
        out_shape=jax.ShapeDtypeStruct((n, GLA_V_WIDTH), BF16),
        scratch_shapes=[pltpu.VMEM((GLA_V_WIDTH, GLA_K_WIDTH), F32),
                        pltpu.VMEM((GLA_V_WIDTH, GLA_K_WIDTH), BF16),
                        pltpu.VMEM((tm, GLA_V_WIDTH), F32)],
        compiler_params=_params("parallel", "arbitrary"),
        name="gla",
    )(g_row, main, kend, dec)


def _mem_kv_kernel(m_ref, w_ref, k_ref, v_ref):
    y = _dot(m_ref[0].astype(BF16), w_ref[...])
    k_ref[0] = y[:, :D_MODEL].astype(BF16)
    v_ref[0] = y[:, D_MODEL:].astype(BF16)


def _mem_kv(mem, wkv):
    batch, m, _ = mem.shape
    blk = pl.BlockSpec((1, m, D_MODEL), lambda b: (b, 0, 0))
    out = jax.ShapeDtypeStruct((batch, m, D_MODEL), BF16)
    return pl.pallas_call(
        _mem_kv_kernel,
        grid=(batch,),
        in_specs=[blk, _const_spec(wkv.shape)],
        out_specs=[blk, blk],
        out_shape=[out, out],
        compiler_params=_params("parallel"),
        name="mem_kv",
    )(mem, wkv)


def _mix_cross_kernel(oda_ref, og_ref, h_ref, wo1_ref, wo2_ref, g2_ref, b2_ref,
                      wq_ref, k_ref, v_ref, wo_ref, g3_ref, b3_ref, o_ref):
    tm = h_ref.shape[0]
    groups = [slice(r * tm // MIX_ROW_GROUPS, (r + 1) * tm // MIX_ROW_GROUPS)
              for r in range(MIX_ROW_GROUPS)]
    mix = [_dot(oda_ref[r, :], wo1_ref[...]) + _dot(og_ref[r, :], wo2_ref[...]) for r in groups]
    h2 = [_layer_norm(ALPHA * h_ref[r, :] + m, g2_ref[...], b2_ref[...]) for r, m in zip(groups, mix)]
    q = [(_dot(x.astype(BF16), wq_ref[...]) * (CROSS_HEAD_DIM ** -0.5)).astype(BF16) for x in h2]
    heads = [[] for _ in groups]
    for h in range(CROSS_HEADS):
        cols = slice(h * CROSS_HEAD_DIM, (h + 1) * CROSS_HEAD_DIM)
        for g, qg in enumerate(q):
            s = lax.dot_general(qg[:, cols], k_ref[0, :, cols], _NT, preferred_element_type=F32)
            p = jnp.exp(s - jnp.max(s, axis=-1, keepdims=True))
            l = jnp.sum(p, axis=-1, keepdims=True)
            heads[g].append((_dot(p.astype(BF16), v_ref[0, :, cols]) / l).astype(BF16))
    for r, x, hs in zip(groups, h2, heads):
        c = _dot(jnp.concatenate(hs, axis=-1), wo_ref[...])
        o_ref[r, :] = _layer_norm(ALPHA * x + c, g3_ref[...], b3_ref[...])


def _mix_cross(o_da, o_g, h1, wo1, wo2, g2, b2, wq, k_mem, v_mem, wo, g3, b3, seq):
    n = h1.shape[0]
    tm = TOKEN_TILE
    per_b = seq // tm
    row = lambda w: pl.BlockSpec((tm, w), lambda i: (i, 0))
    kv = pl.BlockSpec((1,) + k_mem.shape[1:], lambda i: (i // per_b, 0, 0))
    return pl.pallas_call(
        _mix_cross_kernel,
        grid=(n // tm,),
        in_specs=[row(DA_WIDTH), row(GLA_V_WIDTH), row(D_MODEL),
                  _const_spec(wo1.shape), _const_spec(wo2.shape),
                  _const_spec(g2.shape), _const_spec(b2.shape),
                  _const_spec(wq.shape), kv, kv, _const_spec(wo.shape),
                  _const_spec(g3.shape), _const_spec(b3.shape)],
        out_specs=row(D_MODEL),
        out_shape=jax.ShapeDtypeStruct((n, D_MODEL), F32),
        compiler_params=_params("parallel"),
        name="mix_cross",
    )(o_da, o_g, h1, wo1, wo2, g2, b2, wq, k_mem, v_mem, wo, g3, b3)


def kernel(x, mem, ffn1_w_gate, ffn1_w_up, ffn1_w_down, ln1_g, ln1_b, w_in, da_lambda_q1, da_lambda_k1, da_lambda_q2, da_lambda_k2, da_norm_g, gla_w_gate2, gla_b_gate, gla_norm_g, w_out, ln2_g, ln2_b, cross_wq, cross_wkv, cross_wo, ln3_g, ln3_b, ffn2_w_gate, ffn2_w_up, ffn2_w_down, ln4_g, ln4_b):
    batch, seq, d = x.shape
    assert d == D_MODEL and seq % TOKEN_TILE == 0 and TOKEN_TILE % DA_T == 0
    n = batch * seq
    h = x.reshape(n, d)
    row = lambda a, l: a[l].reshape(1, -1)
    for l in range(DEPTH):
        bf = lambda w: w[l].astype(BF16)
        lambda_init = 0.8 - 0.6 * math.exp(-0.3 * l)
        w_in_l = w_in[l]
        pad = LANES - GLA_GATE_RANK
        w_lr = jnp.pad(w_in_l[:, _OFF_GATE:], ((0, 0), (0, pad))).astype(BF16)
        w_g2 = jnp.pad(gla_w_gate2[l], ((0, pad), (0, 0))).astype(BF16)

        h1 = _ffn_ln(h, bf(ffn1_w_gate), bf(ffn1_w_up), bf(ffn1_w_down), row(ln1_g, l), row(ln1_b, l))
        qt, k_da, vt, gla_main, kend, dec = _in_proj(
            h1, w_in_l[:, _OFF_DA:_OFF_GLA].astype(BF16), w_in_l[:, _OFF_GLA:_OFF_GATE].astype(BF16),
            w_lr, w_g2, row(gla_b_gate, l), batch, seq)
        o_da = _diff_attn(qt, k_da.reshape(batch, seq, DA_WIDTH), vt,
                          row(da_lambda_q1, l), row(da_lambda_k1, l),
                          row(da_lambda_q2, l), row(da_lambda_k2, l),
                          da_norm_g[l].reshape(-1, 1), lambda_init)
        o_g = _gla(gla_main, kend, dec, row(gla_norm_g, l), batch, seq)
        k_mem, v_mem = _mem_kv(mem, bf(cross_wkv))
        w_out_l = bf(w_out)
        h3 = _mix_cross(o_da.reshape(n, DA_WIDTH), o_g, h1, w_out_l[:DA_WIDTH], w_out_l[DA_WIDTH:],
                        row(ln2_g, l), row(ln2_b, l), bf(cross_wq), k_mem, v_mem, bf(cross_wo),
                        row(ln3_g, l), row(ln3_b, l), seq)
        h = _ffn_ln(h3, bf(ffn2_w_gate), bf(ffn2_w_up), bf(ffn2_w_down), row(ln4_g, l), row(ln4_b, l))
    return h.reshape(batch, seq, d)
```

```python
import functools
import math

import jax
import jax.numpy as jnp
from jax import lax
from jax.experimental import pallas as pl
from jax.experimental.pallas import tpu as pltpu

D_MODEL = 1024
CHUNK = 64
DA_HEADS = 4
DA_HEAD_DIM = 64
DA_WIDTH = DA_HEADS * 2 * DA_HEAD_DIM
GLA_HEADS = 4
GLA_DK = 64
GLA_DV = 128
GLA_K_WIDTH = GLA_HEADS * GLA_DK
GLA_V_WIDTH = GLA_HEADS * GLA_DV
GLA_GATE_RANK = 16
GLA_GATE_NORM = 16.0
FFN_HIDDEN = 2816
CROSS_HEADS = 4
CROSS_HEAD_DIM = D_MODEL // CROSS_HEADS
DEPTH = 1
ALPHA = (2.0 * DEPTH) ** 0.25
EPS = 1e-5
LANES = 128

_OFF_DA = 0
_OFF_GLA = 3 * DA_WIDTH
_GLA_MAIN = 2 * GLA_K_WIDTH + 2 * GLA_V_WIDTH
_OFF_GATE = _OFF_GLA + _GLA_MAIN

TOKEN_TILE = 512
MXU_DIM = 256
FFN_SPLITS = (0, 1536, FFN_HIDDEN)
assert all(s % MXU_DIM == 0 for s in FFN_SPLITS)
FFN_ROW_GROUPS = 2
MIX_ROW_GROUPS = 2
DA_T = 256
VMEM_LIMIT = 56 * 1024 * 1024

F32 = jnp.float32
BF16 = jnp.bfloat16
_NEG = -1e30
_DA_Q_SCALE = DA_HEAD_DIM ** -0.5 * math.log2(math.e)
_DA_V_PAD = 16
_DA_V_ROWS = 2 * DA_HEAD_DIM + _DA_V_PAD
_DA_UNROLL = 8

_NT = (((1,), (1,)), ((), ()))
_TN = (((0,), (0,)), ((), ()))


def _dot(a, b):
    return jnp.dot(a, b, preferred_element_type=F32)


def _split_bf16(a):
    hi = a.astype(BF16)
    lo = (a - hi.astype(F32)).astype(BF16)
    return hi, lo


def _layer_norm(z, g, b):
    mu = jnp.mean(z, axis=-1, keepdims=True)
    zc = z - mu
    var = jnp.mean(zc * zc, axis=-1, keepdims=True)
    return zc * lax.rsqrt(var + EPS) * g + b


def _const_spec(shape):
    nd = len(shape)
    return pl.BlockSpec(shape, lambda *_: (0,) * nd, pipeline_mode=pl.Buffered(1))


def _params(*sem):
    return pltpu.CompilerParams(dimension_semantics=sem, vmem_limit_bytes=VMEM_LIMIT)


def _ffn_ln_kernel(x_ref, wg_ref, wu_ref, wd_ref, g_ref, b_ref, o_ref):
    tm = x_ref.shape[0]
    for r in range(FFN_ROW_GROUPS):
        rows = slice(r * tm // FFN_ROW_GROUPS, (r + 1) * tm // FFN_ROW_GROUPS)
        x = x_ref[rows, :]
        xb = x.astype(BF16)
        y = None
        for lo, hi in zip(FFN_SPLITS[:-1], FFN_SPLITS[1:]):
            cols = slice(lo, hi)
            gate = _dot(xb, wg_ref[:, cols])
            up = _dot(xb, wu_ref[:, cols])
            hid = (gate * jax.nn.sigmoid(gate) * up).astype(BF16)
            part = _dot(hid, wd_ref[cols, :])
            y = part if y is None else y + part
        o_ref[rows, :] = _layer_norm(ALPHA * x + 0.5 * y, g_ref[...], b_ref[...])


def _ffn_ln(x, wg, wu, wd, g, b):
    n = x.shape[0]
    tm = TOKEN_TILE
    row = pl.BlockSpec((tm, D_MODEL), lambda i: (i, 0))
    return pl.pallas_call(
        _ffn_ln_kernel,
        grid=(n // tm,),
        in_specs=[row, _const_spec(wg.shape), _const_spec(wu.shape), _const_spec(wd.shape),
                  _const_spec(g.shape), _const_spec(b.shape)],
        out_specs=row,
        out_shape=jax.ShapeDtypeStruct((n, D_MODEL), F32),
        compiler_params=_params("parallel"),
        name="ffn_ln",
    )(x, wg, wu, wd, g, b)


def _in_proj_kernel(h_ref, wda_ref, wgla_ref, wlr_ref, wg2_ref, bg_ref, later_ref,
                    qt_ref, k_ref, vt_ref, gla_ref, kend_ref, dec_ref):
    tm = h_ref.shape[0]
    hd2 = 2 * DA_HEAD_DIM
    t = DA_T
    sub = lax.broadcasted_iota(jnp.int32, (_DA_V_PAD, t), 0)
    ones_row = jnp.where(sub == 0, 1.0, 0.0).astype(BF16)

    for j in range(tm // t):
        rows = slice(j * t, (j + 1) * t)
        hb = h_ref[rows, :].astype(BF16)

        g_lr = _dot(hb, wlr_ref[...])
        pre = _dot(g_lr.astype(BF16), wg2_ref[...]) + bg_ref[...]
        log_a = (jnp.minimum(pre, 0.0) - jnp.log(1.0 + jnp.exp(-jnp.abs(pre)))) / GLA_GATE_NORM
        la_hi, la_lo = _split_bf16(log_a)
        rev = _dot(later_ref[...], la_hi) + _dot(later_ref[...], la_lo)
        to_end = jnp.exp(rev)
        total = jnp.sum(log_a.reshape(t // CHUNK, CHUNK, GLA_K_WIDTH), axis=1)
        dec_ref[j * (t // CHUNK):(j + 1) * (t // CHUNK), :] = jnp.exp(total)

        y_g = _dot(hb, wgla_ref[...])
        gla_ref[rows, :] = y_g.astype(BF16)
        kend_ref[rows, :] = (y_g[:, GLA_K_WIDTH:2 * GLA_K_WIDTH] * to_end).astype(BF16)

        y_da = _dot(hb, wda_ref[...])
        k_ref[rows, :] = y_da[:, DA_WIDTH:2 * DA_WIDTH].astype(BF16)
        q_t = (y_da[:, :DA_WIDTH] * _DA_Q_SCALE).T
        v_t = y_da[:, 2 * DA_WIDTH:].T
        for h in range(DA_HEADS):
            head = slice(h * hd2, (h + 1) * hd2)
            qt_ref[0, h, j] = q_t[head, :].astype(BF16)
            vt_ref[0, h, j, :hd2, :] = v_t[head, :].astype(BF16)
            vt_ref[0, h, j, hd2:, :] = ones_row


def _in_proj(h, w_da, w_gla, w_lr, w_g2, b_g, batch, seq):
    n = h.shape[0]
    tm = TOKEN_TILE
    per_b = seq // tm
    row = lambda w: pl.BlockSpec((tm, w), lambda i: (i, 0))
    blocked = lambda i: (i // per_b, 0, i % per_b, 0, 0)
    step = jnp.arange(DA_T)
    later = ((step[None, :] > step[:, None])
             & (step[None, :] // CHUNK == step[:, None] // CHUNK)).astype(BF16)
    return pl.pallas_call(
        _in_proj_kernel,
        grid=(n // tm,),
        in_specs=[row(D_MODEL), _const_spec(w_da.shape), _const_spec(w_gla.shape),
                  _const_spec(w_lr.shape), _const_spec(w_g2.shape), _const_spec(b_g.shape),
                  _const_spec(later.shape)],
        out_specs=[
            pl.BlockSpec((1, DA_HEADS, tm // DA_T, 2 * DA_HEAD_DIM, DA_T), blocked),
            row(DA_WIDTH),
            pl.BlockSpec((1, DA_HEADS, tm // DA_T, _DA_V_ROWS, DA_T), blocked),
            row(_GLA_MAIN),
            row(GLA_K_WIDTH),
            pl.BlockSpec((tm // CHUNK, GLA_K_WIDTH), lambda i: (i, 0)),
        ],
        out_shape=[
            jax.ShapeDtypeStruct((batch, DA_HEADS, seq // DA_T, 2 * DA_HEAD_DIM, DA_T), BF16),
            jax.ShapeDtypeStruct((n, DA_WIDTH), BF16),
            jax.ShapeDtypeStruct((batch, DA_HEADS, seq // DA_T, _DA_V_ROWS, DA_T), BF16),
            jax.ShapeDtypeStruct((n, _GLA_MAIN), BF16),
            jax.ShapeDtypeStruct((n, GLA_K_WIDTH), BF16),
            jax.ShapeDtypeStruct((n // CHUNK, GLA_K_WIDTH), F32),
        ],
        compiler_params=_params("parallel"),
        name="in_proj",
    )(h, w_da, w_gla, w_lr, w_g2, b_g, later)


def _diff_attn_kernel(it_ref, jt_ref, lq1_ref, lk1_ref, lq2_ref, lk2_ref, g_ref, e_ref, mk_ref,
                      qt_ref, k_ref, vt_ref, o_ref, s_a, s_b, p_a, p_b, acc_ref,
                      *, lambda_init, n_items):
    t = DA_T
    hd2 = 2 * DA_HEAD_DIM
    s_bufs = (s_a, s_b)
    p_bufs = (p_a, p_b)
    row = lax.broadcasted_iota(jnp.int32, (hd2, t), 0)
    halves = (row < DA_HEAD_DIM, row >= DA_HEAD_DIM)

    def scores(x, s_ref):
        i, j = it_ref[x], jt_ref[x]
        kj = k_ref[0, pl.ds(pl.multiple_of(j * t, t), t), :]
        k_ext = jnp.concatenate([kj, e_ref[...]], axis=1)
        qt = qt_ref[0, 0, i]
        diag = jnp.where(i == j, 1.0, 0.0)
        mk = (mk_ref[...] * diag).astype(BF16)
        pad = jnp.zeros((hd2 - mk.shape[0], t), BF16)
        top = []
        for c in range(2):
            rhs = jnp.concatenate([jnp.where(halves[c], qt, 0), mk, pad], axis=0)
            s = _dot(k_ext, rhs)
            s_ref[c] = s
            top.append(jnp.max(s, axis=0, keepdims=True))
        return tuple(top)

    def softmax(x, s_ref, p_ref, m, top):
        first = jt_ref[x] == 0
        m_out, a_out = [], []
        for c in range(2):
            m_old = jnp.where(first, _NEG, m[c])
            m_new = jnp.maximum(m_old, top[c])
            p_ref[c] = jnp.exp2(s_ref[c] - m_new).astype(BF16)
            m_out.append(m_new)
            a_out.append(jnp.exp2(m_old - m_new))
        return tuple(m_out), tuple(a_out)

    def values(x, p_ref, a):
        i, j = it_ref[x], jt_ref[x]
        vtj = vt_ref[0, 0, j]
        for c in range(2):
            acc_ref[i, c] = a[c] * acc_ref[i, c] + _dot(vtj, p_ref[c])

    acc_ref[...] = jnp.zeros_like(acc_ref)
    p_b[...] = jnp.zeros_like(p_b)
    top0 = scores(0, s_a)

    def body(step, carry):
        m, a_prev, top = carry
        for u in range(_DA_UNROLL):
            x = step * _DA_UNROLL + u
            top_next = scores(x + 1, s_bufs[(u + 1) % 2])
            m, a = softmax(x, s_bufs[u % 2], p_bufs[u % 2], m, top)
            values(jnp.maximum(x - 1, 0), p_bufs[(u + 1) % 2], a_prev)
            a_prev, top = a, top_next
        return m, a_prev, top

    neg = jnp.full((1, t), _NEG, F32)
    one = jnp.ones((1, t), F32)
    _, a_last, _ = lax.fori_loop(0, n_items // _DA_UNROLL, body, ((neg, neg), (one, one), top0))
    values(n_items - 1, p_bufs[(n_items - 1) % 2], a_last)

    lam = (jnp.exp(jnp.sum(lq1_ref[...] * lk1_ref[...], axis=-1, keepdims=True))
           - jnp.exp(jnp.sum(lq2_ref[...] * lk2_ref[...], axis=-1, keepdims=True))
           + lambda_init)

    def finish(i, _):
        a0, a1 = acc_ref[i, 0], acc_ref[i, 1]
        o = a0[:hd2] / a0[hd2:hd2 + 1] - lam * (a1[:hd2] / a1[hd2:hd2 + 1])
        ms = jnp.mean(o * o, axis=0, keepdims=True)
        y = o * lax.rsqrt(ms + EPS) * g_ref[...] * (1.0 - lambda_init)
        o_ref[0, pl.ds(pl.multiple_of(i * t, t), t), :] = y.T.astype(BF16)
        return 0

    lax.fori_loop(0, acc_ref.shape[0], finish, 0, unroll=2)


def _diff_attn(qt, k, vt, lq1, lk1, lq2, lk2, g_col, lambda_init):
    batch, seq, _ = k.shape
    hd2 = 2 * DA_HEAD_DIM
    t = DA_T
    nq = seq // t
    items = [(i, j) for i in range(nq) for j in range(i + 1)]
    n_items = len(items)
    assert n_items % _DA_UNROLL == 0 and _DA_UNROLL % 2 == 0 and t % CHUNK == 0
    items.append(items[-1])
    i_tab = jnp.asarray([i for i, _ in items], jnp.int32)
    j_tab = jnp.asarray([j for _, j in items], jnp.int32)
    chunk = jnp.arange(t) // CHUNK
    e = (chunk[:, None] == jnp.arange(hd2)[None, :]).astype(BF16)
    ids = jnp.arange(_DA_V_PAD)
    mk = jnp.where((ids[:, None] < t // CHUNK) & (chunk[None, :] < ids[:, None]), _NEG, 0.0).astype(F32)

    const = lambda shape: pl.BlockSpec(shape, lambda *_: (0,) * len(shape), pipeline_mode=pl.Buffered(1))
    head = lambda rows: pl.BlockSpec((1, 1, nq, rows, t), lambda b, h, *_: (b, h, 0, 0, 0))
    return pl.pallas_call(
        functools.partial(_diff_attn_kernel, lambda_init=lambda_init, n_items=n_items),
        grid_spec=pltpu.PrefetchScalarGridSpec(
            num_scalar_prefetch=2,
            grid=(batch, DA_HEADS),
            in_specs=[const(lq1.shape), const(lk1.shape), const(lq2.shape), const(lk2.shape),
                      const(g_col.shape), const(e.shape), const(mk.shape),
                      head(hd2),
                      pl.BlockSpec((1, seq, hd2), lambda b, h, *_: (b, 0, h)),
                      head(_DA_V_ROWS)],
            out_specs=pl.BlockSpec((1, seq, hd2), lambda b, h, *_: (b, 0, h)),
            scratch_shapes=[pltpu.VMEM((2, t, t), F32), pltpu.VMEM((2, t, t), F32),
                            pltpu.VMEM((2, t, t), BF16), pltpu.VMEM((2, t, t), BF16),
                            pltpu.VMEM((nq, 2, _DA_V_ROWS, t), F32)],
        ),
        out_shape=jax.ShapeDtypeStruct((batch, seq, DA_WIDTH), BF16),
        compiler_params=_params("parallel", "parallel"),
        name="diff_attn",
    )(i_tab, j_tab, lq1, lk1, lq2, lk2, g_col, e, mk, qt, k, vt)


def _gla_kernel(g_ref, main_ref, kend_ref, dec_ref, o_ref, st_ref, stb_ref, of_ref):
    @pl.when(pl.program_id(1) == 0)
    def _():
        st_ref[...] = jnp.zeros_like(st_ref)
        stb_ref[...] = jnp.zeros_like(stb_ref)

    tm = main_ref.shape[0]
    for c in range(tm // CHUNK):
        rows = slice(c * CHUNK, (c + 1) * CHUNK)
        q = main_ref[rows, :GLA_K_WIDTH] * (GLA_DK ** -0.5)
        v = main_ref[rows, 2 * GLA_K_WIDTH:2 * GLA_K_WIDTH + GLA_V_WIDTH]
        upd = lax.dot_general(v, kend_ref[rows, :], _TN, preferred_element_type=F32)
        dec = dec_ref[c:c + 1, :]
        for h in range(GLA_HEADS):
            vr = slice(h * GLA_DV, (h + 1) * GLA_DV)
            kc = slice(h * GLA_DK, (h + 1) * GLA_DK)
            new = st_ref[vr, kc] * dec[:, kc] + upd[vr, kc]
            st_ref[vr, kc] = new
            stb_ref[vr, kc] = new.astype(BF16)
        of_ref[rows, :] = lax.dot_general(q.astype(BF16), stb_ref[...], _NT,
                                          preferred_element_type=F32)

    r = main_ref[:, 2 * GLA_K_WIDTH + GLA_V_WIDTH:].astype(F32)
    gate = r * jax.nn.sigmoid(r)
    for h in range(GLA_HEADS):
        vr = slice(h * GLA_DV, (h + 1) * GLA_DV)
        o = of_ref[:, vr]
        ms = jnp.mean(o * o, axis=-1, keepdims=True)
        o_ref[:, vr] = (o * lax.rsqrt(ms + EPS) * g_ref[...] * gate[:, vr]).astype(BF16)


def _gla(main, kend, dec, g_row, batch, seq):
    n = main.shape[0]
    tm = TOKEN_TILE
    per_b = seq // tm
    idx = lambda b, i: (b * per_b + i, 0)
    return pl.pallas_call(
        _gla_kernel,
        grid=(batch, per_b),
        in_specs=[_const_spec(g_row.shape),
                  pl.BlockSpec((tm, _GLA_MAIN), idx),
                  pl.BlockSpec((tm, GLA_K_WIDTH), idx),
                  pl.BlockSpec((tm // CHUNK, GLA_K_WIDTH), idx)],
        out_specs=pl.BlockSpec((tm, GLA_V_WIDTH), idx),
        out_shape=jax.ShapeDtypeStruct((n, GLA_V_WIDTH), BF16),
        scratch_shapes=[pltpu.VMEM((GLA_V_WIDTH, GLA_K_WIDTH), F32),
                        pltpu.VMEM((GLA_V_WIDTH, GLA_K_WIDTH), BF16),
                        pltpu.VMEM((tm, GLA_V_WIDTH), F32)],
        compiler_params=_params("parallel", "arbitrary"),
        name="gla",
    )(g_row, main, kend, dec)


def _mem_kv_kernel(m_ref, w_ref, k_ref, v_ref):
    y = _dot(m_ref[0].astype(BF16), w_ref[...])
    k_ref[0] = y[:, :D_MODEL].astype(BF16)
    v_ref[0] = y[:, D_MODEL:].astype(BF16)


def _mem_kv(mem, wkv):
    batch, m, _ = mem.shape
    blk = pl.BlockSpec((1, m, D_MODEL), lambda b: (b, 0, 0))
    out = jax.ShapeDtypeStruct((batch, m, D_MODEL), BF16)
    return pl.pallas_call(
        _mem_kv_kernel,
        grid=(batch,),
        in_specs=[blk, _const_spec(wkv.shape)],
        out_specs=[blk, blk],
        out_shape=[out, out],
        compiler_params=_params("parallel"),
        name="mem_kv",
    )(mem, wkv)


def _mix_cross_kernel(oda_ref, og_ref, h_ref, wo1_ref, wo2_ref, g2_ref, b2_ref,
                      wq_ref, k_ref, v_ref, wo_ref, g3_ref, b3_ref, o_ref):
    tm = h_ref.shape[0]
    groups = [slice(r * tm // MIX_ROW_GROUPS, (r + 1) * tm // MIX_ROW_GROUPS)
              for r in range(MIX_ROW_GROUPS)]
    mix = [_dot(oda_ref[r, :], wo1_ref[...]) + _dot(og_ref[r, :], wo2_ref[...]) for r in groups]
    h2 = [_layer_norm(ALPHA * h_ref[r, :] + m, g2_ref[...], b2_ref[...]) for r, m in zip(groups, mix)]
    q = [(_dot(x.astype(BF16), wq_ref[...]) * (CROSS_HEAD_DIM ** -0.5)).astype(BF16) for x in h2]
    heads = [[] for _ in groups]
    for h in range(CROSS_HEADS):
        cols = slice(h * CROSS_HEAD_DIM, (h + 1) * CROSS_HEAD_DIM)
        for g, qg in enumerate(q):
            s = lax.dot_general(qg[:, cols], k_ref[0, :, cols], _NT, preferred_element_type=F32)
            p = jnp.exp(s - jnp.max(s, axis=-1, keepdims=True))
            l = jnp.sum(p, axis=-1, keepdims=True)
            heads[g].append((_dot(p.astype(BF16), v_ref[0, :, cols]) / l).astype(BF16))
    for r, x, hs in zip(groups, h2, heads):
        c = _dot(jnp.concatenate(hs, axis=-1), wo_ref[...])
        o_ref[r, :] = _layer_norm(ALPHA * x + c, g3_ref[...], b3_ref[...])


def _mix_cross(o_da, o_g, h1, wo1, wo2, g2, b2, wq, k_mem, v_mem, wo, g3, b3, seq):
    n = h1.shape[0]
    tm = TOKEN_TILE
    per_b = seq // tm
    row = lambda w: pl.BlockSpec((tm, w), lambda i: (i, 0))
    kv = pl.BlockSpec((1,) + k_mem.shape[1:], lambda i: (i // per_b, 0, 0))
    return pl.pallas_call(
        _mix_cross_kernel,
        grid=(n // tm,),
        in_specs=[row(DA_WIDTH), row(GLA_V_WIDTH), row(D_MODEL),
                  _const_spec(wo1.shape), _const_spec(wo2.shape),
                  _const_spec(g2.shape), _const_spec(b2.shape),
                  _const_spec(wq.shape), kv, kv, _const_spec(wo.shape),
                  _const_spec(g3.shape), _const_spec(b3.shape)],
        out_specs=row(D_MODEL),
        out_shape=jax.ShapeDtypeStruct((n, D_MODEL), F32),
        compiler_params=_params("parallel"),
        name="mix_cross",
    )(o_da, o_g, h1, wo1, wo2, g2, b2, wq, k_mem, v_mem, wo, g3, b3)


def kernel(x, mem, ffn1_w_gate, ffn1_w_up, ffn1_w_down, ln1_g, ln1_b, w_in, da_lambda_q1, da_lambda_k1, da_lambda_q2, da_lambda_k2, da_norm_g, gla_w_gate2, gla_b_gate, gla_norm_g, w_out, ln2_g, ln2_b, cross_wq, cross_wkv, cross_wo, ln3_g, ln3_b, ffn2_w_gate, ffn2_w_up, ffn2_w_down, ln4_g, ln4_b):
    batch, seq, d = x.shape
    assert d == D_MODEL and seq % TOKEN_TILE == 0 and TOKEN_TILE % DA_T == 0
    n = batch * seq
    h = x.reshape(n, d)
    row = lambda a, l: a[l].reshape(1, -1)
    for l in range(DEPTH):
        bf = lambda w: w[l].astype(BF16)
        lambda_init = 0.8 - 0.6 * math.exp(-0.3 * l)
        w_in_l = w_in[l]
        pad = LANES - GLA_GATE_RANK
        w_lr = jnp.pad(w_in_l[:, _OFF_GATE:], ((0, 0), (0, pad))).astype(BF16)
        w_g2 = jnp.pad(gla_w_gate2[l], ((0, pad), (0, 0))).astype(BF16)

        h1 = _ffn_ln(h, bf(ffn1_w_gate), bf(ffn1_w_up), bf(ffn1_w_down), row(ln1_g, l), row(ln1_b, l))
        qt, k_da, vt, gla_main, kend, dec = _in_proj(
            h1, w_in_l[:, _OFF_DA:_OFF_GLA].astype(BF16), w_in_l[:, _OFF_GLA:_OFF_GATE].astype(BF16),
            w_lr, w_g2, row(gla_b_gate, l), batch, seq)
        o_da = _diff_attn(qt, k_da.reshape(batch, seq, DA_WIDTH), vt,
                          row(da_lambda_q1, l), row(da_lambda_k1, l),
                          row(da_lambda_q2, l), row(da_lambda_k2, l),
                          da_norm_g[l].reshape(-1, 1), lambda_init)
        o_g = _gla(gla_main, kend, dec, row(gla_norm_g, l), batch, seq)
        k_mem, v_mem = _mem_kv(mem, bf(cross_wkv))
        w_out_l = bf(w_out)
        h3 = _mix_cross(o_da.reshape(n, DA_WIDTH), o_g, h1, w_out_l[:DA_WIDTH], w_out_l[DA_WIDTH:],
                        row(ln2_g, l), row(ln2_b, l), bf(cross_wq), k_mem, v_mem, bf(cross_wo),
                        row(ln3_g, l), row(ln3_b, l), seq)
        h = _ffn_ln(h3, bf(ffn2_w_gate), bf(ffn2_w_up), bf(ffn2_w_down), row(ln4_g, l), row(ln4_b, l))
    return h.reshape(batch, seq, d)
```

```python
import functools
import math

import jax
import jax.numpy as jnp
from jax import lax
from jax.experimental import pallas as pl
from jax.experimental.pallas import tpu as pltpu

D_MODEL = 1024
CHUNK = 64
DA_HEADS = 4
DA_HEAD_DIM = 64
DA_WIDTH = DA_HEADS * 2 * DA_HEAD_DIM
GLA_HEADS = 4
GLA_DK = 64
GLA_DV = 128
GLA_K_WIDTH = GLA_HEADS * GLA_DK
GLA_V_WIDTH = GLA_HEADS * GLA_DV
GLA_GATE_RANK = 16
GLA_GATE_NORM = 16.0
FFN_HIDDEN = 2816
CROSS_HEADS = 4
CROSS_HEAD_DIM = D_MODEL // CROSS_HEADS
DEPTH = 1
ALPHA = (2.0 * DEPTH) ** 0.25
EPS = 1e-5
LANES = 128

_OFF_DA = 0
_OFF_GLA = 3 * DA_WIDTH
_GLA_MAIN = 2 * GLA_K_WIDTH + 2 * GLA_V_WIDTH
_OFF_GATE = _OFF_GLA + _GLA_MAIN

TOKEN_TILE = 512
MXU_DIM = 256
FFN_SPLITS = (0, 1536, FFN_HIDDEN)
assert all(s % MXU_DIM == 0 for s in FFN_SPLITS)
FFN_TILE = 1024
FFN_ROW_GROUPS = 4
MIX_ROW_GROUPS = 2
DA_T = 256
VMEM_LIMIT = 56 * 1024 * 1024

F32 = jnp.float32
BF16 = jnp.bfloat16
_NEG = -1e30
_DA_Q_SCALE = DA_HEAD_DIM ** -0.5 * math.log2(math.e)
_DA_V_PAD = 16
_DA_V_ROWS = 2 * DA_HEAD_DIM + _DA_V_PAD
_DA_UNROLL = 34

_NT = (((1,), (1,)), ((), ()))
_TN = (((0,), (0,)), ((), ()))


def _dot(a, b):
    return jnp.dot(a, b, preferred_element_type=F32)


def _split_bf16(a):
    hi = a.astype(BF16)
    lo = (a - hi.astype(F32)).astype(BF16)
    return hi, lo


def _layer_norm(z, g, b):
    mu = jnp.mean(z, axis=-1, keepdims=True)
    zc = z - mu
    var = jnp.mean(zc * zc, axis=-1, keepdims=True)
    return zc * lax.rsqrt(var + EPS) * g + b


def _const_spec(shape):
    nd = len(shape)
    return pl.BlockSpec(shape, lambda *_: (0,) * nd, pipeline_mode=pl.Buffered(1))


def _params(*sem):
    return pltpu.CompilerParams(dimension_semantics=sem, vmem_limit_bytes=VMEM_LIMIT)


def _ffn_ln_kernel(x_ref, wg_ref, wu_ref, wd_ref, g_ref, b_ref, o_ref):
    tm = x_ref.shape[0]
    for r in range(FFN_ROW_GROUPS):
        rows = slice(r * tm // FFN_ROW_GROUPS, (r + 1) * tm // FFN_ROW_GROUPS)
        x = x_ref[rows, :]
        xb = x.astype(BF16)
        y = None
        for lo, hi in zip(FFN_SPLITS[:-1], FFN_SPLITS[1:]):
            cols = slice(lo, hi)
            gate = _dot(xb, wg_ref[:, cols])
            up = _dot(xb, wu_ref[:, cols])
            hid = (gate * jax.nn.sigmoid(gate) * up).astype(BF16)
            part = _dot(hid, wd_ref[cols, :])
            y = part if y is None else y + part
        o_ref[rows, :] = _layer_norm(ALPHA * x + 0.5 * y, g_ref[...], b_ref[...])


def _ffn_ln(x, wg, wu, wd, g, b):
    n = x.shape[0]
    tm = FFN_TILE
    row = pl.BlockSpec((tm, D_MODEL), lambda i: (i, 0))
    return pl.pallas_call(
        _ffn_ln_kernel,
        grid=(n // tm,),
        in_specs=[row, _const_spec(wg.shape), _const_spec(wu.shape), _const_spec(wd.shape),
                  _const_spec(g.shape), _const_spec(b.shape)],
        out_specs=row,
        out_shape=jax.ShapeDtypeStruct((n, D_MODEL), F32),
        compiler_params=_params("parallel"),
        name="ffn_ln",
    )(x, wg, wu, wd, g, b)


def _in_proj_kernel(h_ref, wda_ref, wgla_ref, wlr_ref, wg2_ref, bg_ref, later_ref,
                    qt_ref, k_ref, vt_ref, gla_ref, kend_ref, dec_ref):
    tm = h_ref.shape[0]
    hd2 = 2 * DA_HEAD_DIM
    t = DA_T
    sub = lax.broadcasted_iota(jnp.int32, (_DA_V_PAD, t), 0)
    ones_row = jnp.where(sub == 0, 1.0, 0.0).astype(BF16)

    for j in range(tm // t):
        rows = slice(j * t, (j + 1) * t)
        hb = h_ref[rows, :].astype(BF16)

        g_lr = _dot(hb, wlr_ref[...])
        pre = _dot(g_lr.astype(BF16), wg2_ref[...]) + bg_ref[...]
        log_a = (jnp.minimum(pre, 0.0) - jnp.log(1.0 + jnp.exp(-jnp.abs(pre)))) / GLA_GATE_NORM
        la_hi, la_lo = _split_bf16(log_a)
        rev = _dot(later_ref[...], la_hi) + _dot(later_ref[...], la_lo)
        to_end = jnp.exp(rev)
        total = jnp.sum(log_a.reshape(t // CHUNK, CHUNK, GLA_K_WIDTH), axis=1)
        dec_ref[j * (t // CHUNK):(j + 1) * (t // CHUNK), :] = jnp.exp(total)

        y_g = _dot(hb, wgla_ref[...])
        gla_ref[rows, :] = y_g.astype(BF16)
        kend_ref[rows, :] = (y_g[:, GLA_K_WIDTH:2 * GLA_K_WIDTH] * to_end).astype(BF16)

        y_da = _dot(hb, wda_ref[...])
        k_ref[rows, :] = y_da[:, DA_WIDTH:2 * DA_WIDTH].astype(BF16)
        q_t = (y_da[:, :DA_WIDTH] * _DA_Q_SCALE).T
        v_t = y_da[:, 2 * DA_WIDTH:].T
        for h in range(DA_HEADS):
            head = slice(h * hd2, (h + 1) * hd2)
            qt_ref[0, h, j] = q_t[head, :].astype(BF16)
            vt_ref[0, h, j, :hd2, :] = v_t[head, :].astype(BF16)
            vt_ref[0, h, j, hd2:, :] = ones_row


def _in_proj(h, w_da, w_gla, w_lr, w_g2, b_g, batch, seq):
    n = h.shape[0]
    tm = TOKEN_TILE
    per_b = seq // tm
    row = lambda w: pl.BlockSpec((tm, w), lambda i: (i, 0))
    blocked = lambda i: (i // per_b, 0, i % per_b, 0, 0)
    step = jnp.arange(DA_T)
    later = ((step[None, :] > step[:, None])
             & (step[None, :] // CHUNK == step[:, None] // CHUNK)).astype(BF16)
    return pl.pallas_call(
        _in_proj_kernel,
        grid=(n // tm,),
        in_specs=[row(D_MODEL), _const_spec(w_da.shape), _const_spec(w_gla.shape),
                  _const_spec(w_lr.shape), _const_spec(w_g2.shape), _const_spec(b_g.shape),
                  _const_spec(later.shape)],
        out_specs=[
            pl.BlockSpec((1, DA_HEADS, tm // DA_T, 2 * DA_HEAD_DIM, DA_T), blocked),
            row(DA_WIDTH),
            pl.BlockSpec((1, DA_HEADS, tm // DA_T, _DA_V_ROWS, DA_T), blocked),
            row(_GLA_MAIN),
            row(GLA_K_WIDTH),
            pl.BlockSpec((tm // CHUNK, GLA_K_WIDTH), lambda i: (i, 0)),
        ],
        out_shape=[
            jax.ShapeDtypeStruct((batch, DA_HEADS, seq // DA_T, 2 * DA_HEAD_DIM, DA_T), BF16),
            jax.ShapeDtypeStruct((n, DA_WIDTH), BF16),
            jax.ShapeDtypeStruct((batch, DA_HEADS, seq // DA_T, _DA_V_ROWS, DA_T), BF16),
            jax.ShapeDtypeStruct((n, _GLA_MAIN), BF16),
            jax.ShapeDtypeStruct((n, GLA_K_WIDTH), BF16),
            jax.ShapeDtypeStruct((n // CHUNK, GLA_K_WIDTH), F32),
        ],
        compiler_params=_params("parallel"),
        name="in_proj",
    )(h, w_da, w_gla, w_lr, w_g2, b_g, later)


def _diff_attn_kernel(it_ref, jt_ref, lq1_ref, lk1_ref, lq2_ref, lk2_ref, g_ref, e_ref, mk_ref,
                      qt_ref, k_ref, vt_ref, o_ref, s_a, s_b, p_a, p_b, acc_ref,
                      *, lambda_init, n_items):
    t = DA_T
    hd2 = 2 * DA_HEAD_DIM
    s_bufs = (s_a, s_b)
    p_bufs = (p_a, p_b)
    row = lax.broadcasted_iota(jnp.int32, (hd2, t), 0)
    halves = (row < DA_HEAD_DIM, row >= DA_HEAD_DIM)

    def scores(x, s_ref):
        i, j = it_ref[x], jt_ref[x]
        kj = k_ref[0, pl.ds(pl.multiple_of(j * t, t), t), :]
        k_ext = jnp.concatenate([kj, e_ref[...]], axis=1)
        qt = qt_ref[0, 0, i]
        diag = jnp.where(i == j, 1.0, 0.0)
        mk = (mk_ref[...] * diag).astype(BF16)
        pad = jnp.zeros((hd2 - mk.shape[0], t), BF16)
        top = []
        for c in range(2):
            rhs = jnp.concatenate([jnp.where(halves[c], qt, 0), mk, pad], axis=0)
            s = _dot(k_ext, rhs)
            s_ref[c] = s
            top.append(jnp.max(s, axis=0, keepdims=True))
        return tuple(top)

    def softmax(x, s_ref, p_ref, m, top):
        first = jt_ref[x] == 0
        m_out, a_out = [], []
        for c in range(2):
            m_old = jnp.where(first, _NEG, m[c])
            m_new = jnp.maximum(m_old, top[c])
            p_ref[c] = jnp.exp2(s_ref[c] - m_new).astype(BF16)
            m_out.append(m_new)
            a_out.append(jnp.exp2(m_old - m_new))
        return tuple(m_out), tuple(a_out)

    def values(x, p_ref, a):
        i, j = it_ref[x], jt_ref[x]
        vtj = vt_ref[0, 0, j]
        for c in range(2):
            acc_ref[i, c] = a[c] * acc_ref[i, c] + _dot(vtj, p_ref[c])

    acc_ref[...] = jnp.zeros_like(acc_ref)
    p_b[...] = jnp.zeros_like(p_b)
    top0 = scores(0, s_a)

    def body(step, carry):
        m, a_prev, top = carry
        for u in range(_DA_UNROLL):
            x = step * _DA_UNROLL + u
            top_next = scores(x + 1, s_bufs[(u + 1) % 2])
            m, a = softmax(x, s_bufs[u % 2], p_bufs[u % 2], m, top)
            values(jnp.maximum(x - 1, 0), p_bufs[(u + 1) % 2], a_prev)
            a_prev, top = a, top_next
        return m, a_prev, top

    neg = jnp.full((1, t), _NEG, F32)
    one = jnp.ones((1, t), F32)
    _, a_last, _ = lax.fori_loop(0, n_items // _DA_UNROLL, body, ((neg, neg), (one, one), top0))
    values(n_items - 1, p_bufs[(n_items - 1) % 2], a_last)

    lam = (jnp.exp(jnp.sum(lq1_ref[...] * lk1_ref[...], axis=-1, keepdims=True))
           - jnp.exp(jnp.sum(lq2_ref[...] * lk2_ref[...], axis=-1, keepdims=True))
           + lambda_init)

    def finish(i, _):
        a0, a1 = acc_ref[i, 0], acc_ref[i, 1]
        o = a0[:hd2] / a0[hd2:hd2 + 1] - lam * (a1[:hd2] / a1[hd2:hd2 + 1])
        ms = jnp.mean(o * o, axis=0, keepdims=True)
        y = o * lax.rsqrt(ms + EPS) * g_ref[...] * (1.0 - lambda_init)
        o_ref[0, pl.ds(pl.multiple_of(i * t, t), t), :] = y.T.astype(BF16)
        return 0

    lax.fori_loop(0, acc_ref.shape[0], finish, 0, unroll=2)


def _diff_attn(qt, k, vt, lq1, lk1, lq2, lk2, g_col, lambda_init):
    batch, seq, _ = k.shape
    hd2 = 2 * DA_HEAD_DIM
    t = DA_T
    nq = seq // t
    items = [(i, j) for i in range(nq) for j in range(i + 1)]
    n_items = len(items)
    assert n_items % _DA_UNROLL == 0 and _DA_UNROLL % 2 == 0 and t % CHUNK == 0
    items.append(items[-1])
    i_tab = jnp.asarray([i for i, _ in items], jnp.int32)
    j_tab = jnp.asarray([j for _, j in items], jnp.int32)
    chunk = jnp.arange(t) // CHUNK
    e = (chunk[:, None] == jnp.arange(hd2)[None, :]).astype(BF16)
    ids = jnp.arange(_DA_V_PAD)
    mk = jnp.where((ids[:, None] < t // CHUNK) & (chunk[None, :] < ids[:, None]), _NEG, 0.0).astype(F32)

    const = lambda shape: pl.BlockSpec(shape, lambda *_: (0,) * len(shape), pipeline_mode=pl.Buffered(1))
    head = lambda rows: pl.BlockSpec((1, 1, nq, rows, t), lambda b, h, *_: (b, h, 0, 0, 0))
    return pl.pallas_call(
        functools.partial(_diff_attn_kernel, lambda_init=lambda_init, n_items=n_items),
        grid_spec=pltpu.PrefetchScalarGridSpec(
            num_scalar_prefetch=2,
            grid=(batch, DA_HEADS),
            in_specs=[const(lq1.shape), const(lk1.shape), const(lq2.shape), const(lk2.shape),
                      const(g_col.shape), const(e.shape), const(mk.shape),
                      head(hd2),
                      pl.BlockSpec((1, seq, hd2), lambda b, h, *_: (b, 0, h)),
                      head(_DA_V_ROWS)],
            out_specs=pl.BlockSpec((1, seq, hd2), lambda b, h, *_: (b, 0, h)),
            scratch_shapes=[pltpu.VMEM((2, t, t), F32), pltpu.VMEM((2, t, t), F32),
                            pltpu.VMEM((2, t, t), BF16), pltpu.VMEM((2, t, t), BF16),
                            pltpu.VMEM((nq, 2, _DA_V_ROWS, t), F32)],
        ),
        out_shape=jax.ShapeDtypeStruct((batch, seq, DA_WIDTH), BF16),
        compiler_params=_params("parallel", "parallel"),
        name="diff_attn",
    )(i_tab, j_tab, lq1, lk1, lq2, lk2, g_col, e, mk, qt, k, vt)


def _gla_kernel(g_ref, main_ref, kend_ref, dec_ref, o_ref, st_ref, stb_ref, upd_ref, gate_ref):
    @pl.when(pl.program_id(1) == 0)
    def _():
        st_ref[...] = jnp.zeros_like(st_ref)
        stb_ref[...] = jnp.zeros_like(stb_ref)

    tm = main_ref.shape[0]
    chunks = [slice(c * CHUNK, (c + 1) * CHUNK) for c in range(tm // CHUNK)]
    blocks = [(slice(h * GLA_DV, (h + 1) * GLA_DV), slice(h * GLA_DK, (h + 1) * GLA_DK))
              for h in range(GLA_HEADS)]

    for c, rows in enumerate(chunks):
        r = main_ref[rows, 2 * GLA_K_WIDTH + GLA_V_WIDTH:].astype(F32)
        gate_ref[rows, :] = r * jax.nn.sigmoid(r)
        v = main_ref[rows, 2 * GLA_K_WIDTH:2 * GLA_K_WIDTH + GLA_V_WIDTH]
        upd = lax.dot_general(v, kend_ref[rows, :], _TN, preferred_element_type=F32)
        for vr, kc in blocks:
            upd_ref[c, vr, kc] = upd[vr, kc]
    for c in range(len(chunks)):
        dec = dec_ref[c:c + 1, :]
        for vr, kc in blocks:
            new = st_ref[vr, kc] * dec[:, kc] + upd_ref[c, vr, kc]
            st_ref[vr, kc] = new
            stb_ref[c, vr, kc] = new.astype(BF16)
    for c, rows in enumerate(chunks):
        q = main_ref[rows, :GLA_K_WIDTH] * (GLA_DK ** -0.5)
        out = lax.dot_general(q.astype(BF16), stb_ref[c], _NT,
                              preferred_element_type=F32)
        for vr, _ in blocks:
            o = out[:, vr]
            ms = jnp.mean(o * o, axis=-1, keepdims=True)
            o_ref[rows, vr] = (o * lax.rsqrt(ms + EPS) * g_ref[...] * gate_ref[rows, vr]).astype(BF16)


def _gla(main, kend, dec, g_row, batch, seq):
    n = main.shape[0]
    tm = TOKEN_TILE
    per_b = seq // tm
    idx = lambda b, i: (b * per_b + i, 0)
    return pl.pallas_call(
        _gla_kernel,
        grid=(batch, per_b),
        in_specs=[_const_spec(g_row.shape),
                  pl.BlockSpec((tm, _GLA_MAIN), idx),
                  pl.BlockSpec((tm, GLA_K_WIDTH), idx),
                  pl.BlockSpec((tm // CHUNK, GLA_K_WIDTH), idx)],
        out_specs=pl.BlockSpec((tm, GLA_V_WIDTH), idx),
        out_shape=jax.ShapeDtypeStruct((n, GLA_V_WIDTH), BF16),
        scratch_shapes=[pltpu.VMEM((GLA_V_WIDTH, GLA_K_WIDTH), F32),
                        pltpu.VMEM((tm // CHUNK, GLA_V_WIDTH, GLA_K_WIDTH), BF16),
                        pltpu.VMEM((tm // CHUNK, GLA_V_WIDTH, GLA_K_WIDTH), F32),
                        pltpu.VMEM((tm, GLA_V_WIDTH), F32)],
        compiler_params=_params("parallel", "arbitrary"),
        name="gla",
    )(g_row, main, kend, dec)


def _mem_kv_kernel(m_ref, w_ref, k_ref, v_ref):
    y = _dot(m_ref[0].astype(BF16), w_ref[...])
    k_ref[0] = y[:, :D_MODEL].astype(BF16)
    v_ref[0] = y[:, D_MODEL:].astype(BF16)


def _mem_kv(mem, wkv):
    batch, m, _ = mem.shape
    blk = pl.BlockSpec((1, m, D_MODEL), lambda b: (b, 0, 0))
    out = jax.ShapeDtypeStruct((batch, m, D_MODEL), BF16)
    return pl.pallas_call(
        _mem_kv_kernel,
        grid=(batch,),
        in_specs=[blk, _const_spec(wkv.shape)],
        out_specs=[blk, blk],
        out_shape=[out, out],
        compiler_params=_params("parallel"),
        name="mem_kv",
    )(mem, wkv)


def _mix_cross_kernel(oda_ref, og_ref, h_ref, wo1_ref, wo2_ref, g2_ref, b2_ref,
                      wq_ref, k_ref, v_ref, wo_ref, g3_ref, b3_ref, o_ref):
    tm = h_ref.shape[0]
    groups = [slice(r * tm // MIX_ROW_GROUPS, (r + 1) * tm // MIX_ROW_GROUPS)
              for r in range(MIX_ROW_GROUPS)]
    mix = [_dot(oda_ref[r, :], wo1_ref[...]) + _dot(og_ref[r, :], wo2_ref[...]) for r in groups]
    h2 = [_layer_norm(ALPHA * h_ref[r, :] + m, g2_ref[...], b2_ref[...]) for r, m in zip(groups, mix)]
    q = [(_dot(x.astype(BF16), wq_ref[...]) * (CROSS_HEAD_DIM ** -0.5)).astype(BF16) for x in h2]
    heads = [[] for _ in groups]
    for h in range(CROSS_HEADS):
        cols = slice(h * CROSS_HEAD_DIM, (h + 1) * CROSS_HEAD_DIM)
        for g, qg in enumerate(q):
            s = lax.dot_general(qg[:, cols], k_ref[0, :, cols], _NT, preferred_element_type=F32)
            p = jnp.exp(s - jnp.max(s, axis=-1, keepdims=True))
            l = jnp.sum(p, axis=-1, keepdims=True)
            heads[g].append((_dot(p.astype(BF16), v_ref[0, :, cols]) / l).astype(BF16))
    for r, x, hs in zip(groups, h2, heads):
        c = _dot(jnp.concatenate(hs, axis=-1), wo_ref[...])
        o_ref[r, :] = _layer_norm(ALPHA * x + c, g3_ref[...], b3_ref[...])


def _mix_cross(o_da, o_g, h1, wo1, wo2, g2, b2, wq, k_mem, v_mem, wo, g3, b3, seq):
    n = h1.shape[0]
    tm = TOKEN_TILE
    per_b = seq // tm
    row = lambda w: pl.BlockSpec((tm, w), lambda i: (i, 0))
    kv = pl.BlockSpec((1,) + k_mem.shape[1:], lambda i: (i // per_b, 0, 0))
    return pl.pallas_call(
        _mix_cross_kernel,
        grid=(n // tm,),
        in_specs=[row(DA_WIDTH), row(GLA_V_WIDTH), row(D_MODEL),
                  _const_spec(wo1.shape), _const_spec(wo2.shape),
                  _const_spec(g2.shape), _const_spec(b2.shape),
                  _const_spec(wq.shape), kv, kv, _const_spec(wo.shape),
                  _const_spec(g3.shape), _const_spec(b3.shape)],
        out_specs=row(D_MODEL),
        out_shape=jax.ShapeDtypeStruct((n, D_MODEL), F32),
        compiler_params=_params("parallel"),
        name="mix_cross",
    )(o_da, o_g, h1, wo1, wo2, g2, b2, wq, k_mem, v_mem, wo, g3, b3)


def kernel(x, mem, ffn1_w_gate, ffn1_w_up, ffn1_w_down, ln1_g, ln1_b, w_in, da_lambda_q1, da_lambda_k1, da_lambda_q2, da_lambda_k2, da_norm_g, gla_w_gate2, gla_b_gate, gla_norm_g, w_out, ln2_g, ln2_b, cross_wq, cross_wkv, cross_wo, ln3_g, ln3_b, ffn2_w_gate, ffn2_w_up, ffn2_w_down, ln4_g, ln4_b):
    batch, seq, d = x.shape
    assert d == D_MODEL and seq % TOKEN_TILE == 0 and TOKEN_TILE % DA_T == 0
    assert (batch * seq) % FFN_TILE == 0 and FFN_TILE % FFN_ROW_GROUPS == 0
    n = batch * seq
    h = x.reshape(n, d)
    row = lambda a, l: a[l].reshape(1, -1)
    for l in range(DEPTH):
        bf = lambda w: w[l].astype(BF16)
        lambda_init = 0.8 - 0.6 * math.exp(-0.3 * l)
        w_in_l = w_in[l]
        pad = LANES - GLA_GATE_RANK
        w_lr = jnp.pad(w_in_l[:, _OFF_GATE:], ((0, 0), (0, pad))).astype(BF16)
        w_g2 = jnp.pad(gla_w_gate2[l], ((0, pad), (0, 0))).astype(BF16)

        h1 = _ffn_ln(h, bf(ffn1_w_gate), bf(ffn1_w_up), bf(ffn1_w_down), row(ln1_g, l), row(ln1_b, l))
        qt, k_da, vt, gla_main, kend, dec = _in_proj(
            h1, w_in_l[:, _OFF_DA:_OFF_GLA].astype(BF16), w_in_l[:, _OFF_GLA:_OFF_GATE].astype(BF16),
            w_lr, w_g2, row(gla_b_gate, l), batch, seq)
        o_da = _diff_attn(qt, k_da.reshape(batch, seq, DA_WIDTH), vt,
                          row(da_lambda_q1, l), row(da_lambda_k1, l),
                          row(da_lambda_q2, l), row(da_lambda_k2, l),
                          da_norm_g[l].reshape(-1, 1), lambda_init)
        o_g = _gla(gla_main, kend, dec, row(gla_norm_g, l), batch, seq)
        k_mem, v_mem = _mem_kv(mem, bf(cross_wkv))
        w_out_l = bf(w_out)
        h3 = _mix_cross(o_da.reshape(n, DA_WIDTH), o_g, h1, w_out_l[:DA_WIDTH], w_out_l[DA_WIDTH:],
                        row(ln2_g, l), row(ln2_b, l), bf(cross_wq), k_mem, v_mem, bf(cross_wo),
                        row(ln3_g, l), row(ln3_b, l), seq)
        h = _ffn_ln(h3, bf(ffn2_w_gate), bf(ffn2_w_up), bf(ffn2_w_down), row(ln4_g, l), row(ln4_b, l))
    return h.reshape(batch, seq, d)
```

```python
import functools
import math

import jax
import jax.numpy as jnp
from jax import lax
from jax.experimental import pallas as pl
from jax.experimental.pallas import tpu as pltpu

D_MODEL = 1024
CHUNK = 64
DA_HEADS = 4
DA_HEAD_DIM = 64
DA_WIDTH = DA_HEADS * 2 * DA_HEAD_DIM
GLA_HEADS = 4
GLA_DK = 64
GLA_DV = 128
GLA_K_WIDTH = GLA_HEADS * GLA_DK
GLA_V_WIDTH = GLA_HEADS * GLA_DV
GLA_GATE_RANK = 16
GLA_GATE_NORM = 16.0
FFN_HIDDEN = 2816
CROSS_HEADS = 4
CROSS_HEAD_DIM = D_MODEL // CROSS_HEADS
DEPTH = 1
ALPHA = (2.0 * DEPTH) ** 0.25
EPS = 1e-5
LANES = 128

_OFF_DA = 0
_OFF_GLA = 3 * DA_WIDTH
_GLA_MAIN = 2 * GLA_K_WIDTH + 2 * GLA_V_WIDTH
_OFF_GATE = _OFF_GLA + _GLA_MAIN

TOKEN_TILE = 512
MXU_DIM = 256
FFN_SPLITS = (0, 1536, FFN_HIDDEN)
assert all(s % MXU_DIM == 0 for s in FFN_SPLITS)
FFN_TILE = 1024
FFN_ROW_GROUPS = 4
FFN_CAST_STEPS = 8
MIX_ROW_GROUPS = 2
DA_T = 256
VMEM_LIMIT = 56 * 1024 * 1024

F32 = jnp.float32
BF16 = jnp.bfloat16
_NEG = -1e30
_DA_Q_SCALE = DA_HEAD_DIM ** -0.5 * math.log2(math.e)
_DA_V_PAD = 16
_DA_V_ROWS = 2 * DA_HEAD_DIM + _DA_V_PAD
_DA_UNROLL = 34

_NT = (((1,), (1,)), ((), ()))
_TN = (((0,), (0,)), ((), ()))


def _dot(a, b):
    return jnp.dot(a, b, preferred_element_type=F32)


def _split_bf16(a):
    hi = a.astype(BF16)
    lo = (a - hi.astype(F32)).astype(BF16)
    return hi, lo


def _layer_norm(z, g, b):
    mu = jnp.mean(z, axis=-1, keepdims=True)
    zc = z - mu
    var = jnp.mean(zc * zc, axis=-1, keepdims=True)
    return zc * lax.rsqrt(var + EPS) * g + b


def _const_spec(shape):
    nd = len(shape)
    return pl.BlockSpec(shape, lambda *_: (0,) * nd, pipeline_mode=pl.Buffered(1))


def _params(*sem):
    return pltpu.CompilerParams(dimension_semantics=sem, vmem_limit_bytes=VMEM_LIMIT)


def _ffn_ln_kernel(x_ref, wg32_ref, wu32_ref, wd32_ref, g_ref, b_ref, o_ref, wg_ref, wu_ref, wd_ref):
    i = pl.program_id(0)

    @pl.when(i < FFN_CAST_STEPS)
    def _():
        for src, dst in ((wg32_ref, wg_ref), (wu32_ref, wu_ref), (wd32_ref, wd_ref)):
            rows = src.shape[0]
            dst[pl.ds(pl.multiple_of(i * rows, rows), rows), :] = src[...].astype(BF16)

    @pl.when(i >= FFN_CAST_STEPS)
    def _():
        tm = x_ref.shape[0]
        for r in range(FFN_ROW_GROUPS):
            rows = slice(r * tm // FFN_ROW_GROUPS, (r + 1) * tm // FFN_ROW_GROUPS)
            x = x_ref[rows, :]
            xb = x.astype(BF16)
            y = None
            for lo, hi in zip(FFN_SPLITS[:-1], FFN_SPLITS[1:]):
                cols = slice(lo, hi)
                gate = _dot(xb, wg_ref[:, cols])
                up = _dot(xb, wu_ref[:, cols])
                hid = (gate * jax.nn.sigmoid(gate) * up).astype(BF16)
                part = _dot(hid, wd_ref[cols, :])
                y = part if y is None else y + part
            o_ref[rows, :] = _layer_norm(ALPHA * x + 0.5 * y, g_ref[...], b_ref[...])


def _ffn_ln(x, wg, wu, wd, g, b):
    n = x.shape[0]
    tm = FFN_TILE
    nc = FFN_CAST_STEPS
    row = pl.BlockSpec((tm, D_MODEL), lambda i: (jnp.maximum(i - nc, 0), 0))
    w_chunk = lambda w: pl.BlockSpec((w.shape[0] // nc, w.shape[1]), lambda i: (jnp.minimum(i, nc - 1), 0))
    return pl.pallas_call(
        _ffn_ln_kernel,
        grid=(nc + n // tm,),
        in_specs=[row, w_chunk(wg), w_chunk(wu), w_chunk(wd),
                  _const_spec(g.shape), _const_spec(b.shape)],
        out_specs=row,
        out_shape=jax.ShapeDtypeStruct((n, D_MODEL), F32),
        scratch_shapes=[pltpu.VMEM(wg.shape, BF16), pltpu.VMEM(wu.shape, BF16), pltpu.VMEM(wd.shape, BF16)],
        compiler_params=_params("arbitrary"),
        name="ffn_ln",
    )(x, wg, wu, wd, g, b)


def _in_proj_kernel(h_ref, wda_ref, wgla_ref, wlr_ref, wg2_ref, bg_ref, later_ref,
                    qt_ref, k_ref, vt_ref, gla_ref, kend_ref, dec_ref):
    tm = h_ref.shape[0]
    hd2 = 2 * DA_HEAD_DIM
    t = DA_T
    sub = lax.broadcasted_iota(jnp.int32, (_DA_V_PAD, t), 0)
    ones_row = jnp.where(sub == 0, 1.0, 0.0).astype(BF16)

    for j in range(tm // t):
        rows = slice(j * t, (j + 1) * t)
        hb = h_ref[rows, :].astype(BF16)

        g_lr = _dot(hb, wlr_ref[...])
        pre = _dot(g_lr.astype(BF16), wg2_ref[...]) + bg_ref[...]
        log_a = (jnp.minimum(pre, 0.0) - jnp.log(1.0 + jnp.exp(-jnp.abs(pre)))) / GLA_GATE_NORM
        la_hi, la_lo = _split_bf16(log_a)
        rev = _dot(later_ref[...], la_hi) + _dot(later_ref[...], la_lo)
        to_end = jnp.exp(rev)
        total = jnp.sum(log_a.reshape(t // CHUNK, CHUNK, GLA_K_WIDTH), axis=1)
        dec_ref[j * (t // CHUNK):(j + 1) * (t // CHUNK), :] = jnp.exp(total)

        y_g = _dot(hb, wgla_ref[...])
        gla_ref[rows, :] = y_g.astype(BF16)
        kend_ref[rows, :] = (y_g[:, GLA_K_WIDTH:2 * GLA_K_WIDTH] * to_end).astype(BF16)

        y_da = _dot(hb, wda_ref[...])
        k_ref[rows, :] = y_da[:, DA_WIDTH:2 * DA_WIDTH].astype(BF16)
        q_t = (y_da[:, :DA_WIDTH] * _DA_Q_SCALE).T
        v_t = y_da[:, 2 * DA_WIDTH:].T
        for h in range(DA_HEADS):
            head = slice(h * hd2, (h + 1) * hd2)
            qt_ref[0, h, j] = q_t[head, :].astype(BF16)
            vt_ref[0, h, j, :hd2, :] = v_t[head, :].astype(BF16)
            vt_ref[0, h, j, hd2:, :] = ones_row


def _in_proj(h, w_da, w_gla, w_lr, w_g2, b_g, batch, seq):
    n = h.shape[0]
    tm = TOKEN_TILE
    per_b = seq // tm
    row = lambda w: pl.BlockSpec((tm, w), lambda i: (i, 0))
    blocked = lambda i: (i // per_b, 0, i % per_b, 0, 0)
    step = jnp.arange(DA_T)
    later = ((step[None, :] > step[:, None])
             & (step[None, :] // CHUNK == step[:, None] // CHUNK)).astype(BF16)
    return pl.pallas_call(
        _in_proj_kernel,
        grid=(n // tm,),
        in_specs=[row(D_MODEL), _const_spec(w_da.shape), _const_spec(w_gla.shape),
                  _const_spec(w_lr.shape), _const_spec(w_g2.shape), _const_spec(b_g.shape),
                  _const_spec(later.shape)],
        out_specs=[
            pl.BlockSpec((1, DA_HEADS, tm // DA_T, 2 * DA_HEAD_DIM, DA_T), blocked),
            row(DA_WIDTH),
            pl.BlockSpec((1, DA_HEADS, tm // DA_T, _DA_V_ROWS, DA_T), blocked),
            row(_GLA_MAIN),
            row(GLA_K_WIDTH),
            pl.BlockSpec((tm // CHUNK, GLA_K_WIDTH), lambda i: (i, 0)),
        ],
        out_shape=[
            jax.ShapeDtypeStruct((batch, DA_HEADS, seq // DA_T, 2 * DA_HEAD_DIM, DA_T), BF16),
            jax.ShapeDtypeStruct((n, DA_WIDTH), BF16),
            jax.ShapeDtypeStruct((batch, DA_HEADS, seq // DA_T, _DA_V_ROWS, DA_T), BF16),
            jax.ShapeDtypeStruct((n, _GLA_MAIN), BF16),
            jax.ShapeDtypeStruct((n, GLA_K_WIDTH), BF16),
            jax.ShapeDtypeStruct((n // CHUNK, GLA_K_WIDTH), F32),
        ],
        compiler_params=_params("parallel"),
        name="in_proj",
    )(h, w_da, w_gla, w_lr, w_g2, b_g, later)


def _diff_attn_kernel(it_ref, jt_ref, lq1_ref, lk1_ref, lq2_ref, lk2_ref, g_ref, e_ref, mk_ref,
                      qt_ref, k_ref, vt_ref, o_ref, s_a, s_b, p_a, p_b, acc_ref,
                      *, lambda_init, n_items):
    t = DA_T
    hd2 = 2 * DA_HEAD_DIM
    s_bufs = (s_a, s_b)
    p_bufs = (p_a, p_b)
    row = lax.broadcasted_iota(jnp.int32, (hd2, t), 0)
    halves = (row < DA_HEAD_DIM, row >= DA_HEAD_DIM)

    def scores(x, s_ref):
        i, j = it_ref[x], jt_ref[x]
        kj = k_ref[0, pl.ds(pl.multiple_of(j * t, t), t), :]
        k_ext = jnp.concatenate([kj, e_ref[...]], axis=1)
        qt = qt_ref[0, 0, i]
        diag = jnp.where(i == j, 1.0, 0.0)
        mk = (mk_ref[...] * diag).astype(BF16)
        pad = jnp.zeros((hd2 - mk.shape[0], t), BF16)
        top = []
        for c in range(2):
            rhs = jnp.concatenate([jnp.where(halves[c], qt, 0), mk, pad], axis=0)
            s = _dot(k_ext, rhs)
            s_ref[c] = s
            top.append(jnp.max(s, axis=0, keepdims=True))
        return tuple(top)

    def softmax(x, s_ref, p_ref, m, top):
        first = jt_ref[x] == 0
        m_out, a_out = [], []
        for c in range(2):
            m_old = jnp.where(first, _NEG, m[c])
            m_new = jnp.maximum(m_old, top[c])
            p_ref[c] = jnp.exp2(s_ref[c] - m_new).astype(BF16)
            m_out.append(m_new)
            a_out.append(jnp.exp2(m_old - m_new))
        return tuple(m_out), tuple(a_out)

    def values(x, p_ref, a):
        i, j = it_ref[x], jt_ref[x]
        vtj = vt_ref[0, 0, j]
        for c in range(2):
            acc_ref[i, c] = a[c] * acc_ref[i, c] + _dot(vtj, p_ref[c])

    acc_ref[...] = jnp.zeros_like(acc_ref)
    p_b[...] = jnp.zeros_like(p_b)
    top0 = scores(0, s_a)

    def body(step, carry):
        m, a_prev, top = carry
        for u in range(_DA_UNROLL):
            x = step * _DA_UNROLL + u
            top_next = scores(x + 1, s_bufs[(u + 1) % 2])
            m, a = softmax(x, s_bufs[u % 2], p_bufs[u % 2], m, top)
            values(jnp.maximum(x - 1, 0), p_bufs[(u + 1) % 2], a_prev)
            a_prev, top = a, top_next
        return m, a_prev, top

    neg = jnp.full((1, t), _NEG, F32)
    one = jnp.ones((1, t), F32)
    _, a_last, _ = lax.fori_loop(0, n_items // _DA_UNROLL, body, ((neg, neg), (one, one), top0))
    values(n_items - 1, p_bufs[(n_items - 1) % 2], a_last)

    lam = (jnp.exp(jnp.sum(lq1_ref[...] * lk1_ref[...], axis=-1, keepdims=True))
           - jnp.exp(jnp.sum(lq2_ref[...] * lk2_ref[...], axis=-1, keepdims=True))
           + lambda_init)

    def finish(i, _):
        a0, a1 = acc_ref[i, 0], acc_ref[i, 1]
        o = a0[:hd2] / a0[hd2:hd2 + 1] - lam * (a1[:hd2] / a1[hd2:hd2 + 1])
        ms = jnp.mean(o * o, axis=0, keepdims=True)
        y = o * lax.rsqrt(ms + EPS) * g_ref[...] * (1.0 - lambda_init)
        o_ref[0, pl.ds(pl.multiple_of(i * t, t), t), :] = y.T.astype(BF16)
        return 0

    lax.fori_loop(0, acc_ref.shape[0], finish, 0, unroll=2)


def _diff_attn(qt, k, vt, lq1, lk1, lq2, lk2, g_col, lambda_init):
    batch, seq, _ = k.shape
    hd2 = 2 * DA_HEAD_DIM
    t = DA_T
    nq = seq // t
    items = [(i, j) for i in range(nq) for j in range(i + 1)]
    n_items = len(items)
    assert n_items % _DA_UNROLL == 0 and _DA_UNROLL % 2 == 0 and t % CHUNK == 0
    items.append(items[-1])
    i_tab = jnp.asarray([i for i, _ in items], jnp.int32)
    j_tab = jnp.asarray([j for _, j in items], jnp.int32)
    chunk = jnp.arange(t) // CHUNK
    e = (chunk[:, None] == jnp.arange(hd2)[None, :]).astype(BF16)
    ids = jnp.arange(_DA_V_PAD)
    mk = jnp.where((ids[:, None] < t // CHUNK) & (chunk[None, :] < ids[:, None]), _NEG, 0.0).astype(F32)

    const = lambda shape: pl.BlockSpec(shape, lambda *_: (0,) * len(shape), pipeline_mode=pl.Buffered(1))
    head = lambda rows: pl.BlockSpec((1, 1, nq, rows, t), lambda b, h, *_: (b, h, 0, 0, 0))
    return pl.pallas_call(
        functools.partial(_diff_attn_kernel, lambda_init=lambda_init, n_items=n_items),
        grid_spec=pltpu.PrefetchScalarGridSpec(
            num_scalar_prefetch=2,
            grid=(batch, DA_HEADS),
            in_specs=[const(lq1.shape), const(lk1.shape), const(lq2.shape), const(lk2.shape),
                      const(g_col.shape), const(e.shape), const(mk.shape),
                      head(hd2),
                      pl.BlockSpec((1, seq, hd2), lambda b, h, *_: (b, 0, h)),
                      head(_DA_V_ROWS)],
            out_specs=pl.BlockSpec((1, seq, hd2), lambda b, h, *_: (b, 0, h)),
            scratch_shapes=[pltpu.VMEM((2, t, t), F32), pltpu.VMEM((2, t, t), F32),
                            pltpu.VMEM((2, t, t), BF16), pltpu.VMEM((2, t, t), BF16),
                            pltpu.VMEM((nq, 2, _DA_V_ROWS, t), F32)],
        ),
        out_shape=jax.ShapeDtypeStruct((batch, seq, DA_WIDTH), BF16),
        compiler_params=_params("parallel", "parallel"),
        name="diff_attn",
    )(i_tab, j_tab, lq1, lk1, lq2, lk2, g_col, e, mk, qt, k, vt)


def _gla_kernel(g_ref, main_ref, kend_ref, dec_ref, o_ref, st_ref, stb_ref, upd_ref, gate_ref):
    @pl.when(pl.program_id(1) == 0)
    def _():
        st_ref[...] = jnp.zeros_like(st_ref)
        stb_ref[...] = jnp.zeros_like(stb_ref)

    tm = main_ref.shape[0]
    chunks = [slice(c * CHUNK, (c + 1) * CHUNK) for c in range(tm // CHUNK)]
    blocks = [(slice(h * GLA_DV, (h + 1) * GLA_DV), slice(h * GLA_DK, (h + 1) * GLA_DK))
              for h in range(GLA_HEADS)]

    for c, rows in enumerate(chunks):
        r = main_ref[rows, 2 * GLA_K_WIDTH + GLA_V_WIDTH:].astype(F32)
        gate_ref[rows, :] = r * jax.nn.sigmoid(r)
        v = main_ref[rows, 2 * GLA_K_WIDTH:2 * GLA_K_WIDTH + GLA_V_WIDTH]
        upd = lax.dot_general(v, kend_ref[rows, :], _TN, preferred_element_type=F32)
        for vr, kc in blocks:
            upd_ref[c, vr, kc] = upd[vr, kc]
    for c in range(len(chunks)):
        dec = dec_ref[c:c + 1, :]
        for vr, kc in blocks:
            new = st_ref[vr, kc] * dec[:, kc] + upd_ref[c, vr, kc]
            st_ref[vr, kc] = new
            stb_ref[c, vr, kc] = new.astype(BF16)
    for c, rows in enumerate(chunks):
        q = main_ref[rows, :GLA_K_WIDTH] * (GLA_DK ** -0.5)
        out = lax.dot_general(q.astype(BF16), stb_ref[c], _NT,
                              preferred_element_type=F32)
        for vr, _ in blocks:
            o = out[:, vr]
            ms = jnp.mean(o * o, axis=-1, keepdims=True)
            o_ref[rows, vr] = (o * lax.rsqrt(ms + EPS) * g_ref[...] * gate_ref[rows, vr]).astype(BF16)


def _gla(main, kend, dec, g_row, batch, seq):
    n = main.shape[0]
    tm = TOKEN_TILE
    per_b = seq // tm
    idx = lambda b, i: (b * per_b + i, 0)
    return pl.pallas_call(
        _gla_kernel,
        grid=(batch, per_b),
        in_specs=[_const_spec(g_row.shape),
                  pl.BlockSpec((tm, _GLA_MAIN), idx),
                  pl.BlockSpec((tm, GLA_K_WIDTH), idx),
                  pl.BlockSpec((tm // CHUNK, GLA_K_WIDTH), idx)],
        out_specs=pl.BlockSpec((tm, GLA_V_WIDTH), idx),
        out_shape=jax.ShapeDtypeStruct((n, GLA_V_WIDTH), BF16),
        scratch_shapes=[pltpu.VMEM((GLA_V_WIDTH, GLA_K_WIDTH), F32),
                        pltpu.VMEM((tm // CHUNK, GLA_V_WIDTH, GLA_K_WIDTH), BF16),
                        pltpu.VMEM((tm // CHUNK, GLA_V_WIDTH, GLA_K_WIDTH), F32),
                        pltpu.VMEM((tm, GLA_V_WIDTH), F32)],
        compiler_params=_params("parallel", "arbitrary"),
        name="gla",
    )(g_row, main, kend, dec)


def _mem_kv_kernel(m_ref, w_ref, k_ref, v_ref):
    y = _dot(m_ref[0].astype(BF16), w_ref[...])
    k_ref[0] = y[:, :D_MODEL].astype(BF16)
    v_ref[0] = y[:, D_MODEL:].astype(BF16)


def _mem_kv(mem, wkv):
    batch, m, _ = mem.shape
    blk = pl.BlockSpec((1, m, D_MODEL), lambda b: (b, 0, 0))
    out = jax.ShapeDtypeStruct((batch, m, D_MODEL), BF16)
    return pl.pallas_call(
        _mem_kv_kernel,
        grid=(batch,),
        in_specs=[blk, _const_spec(wkv.shape)],
        out_specs=[blk, blk],
        out_shape=[out, out],
        compiler_params=_params("parallel"),
        name="mem_kv",
    )(mem, wkv)


def _mix_cross_kernel(oda_ref, og_ref, h_ref, wo1_ref, wo2_ref, g2_ref, b2_ref,
                      wq_ref, k_ref, v_ref, wo_ref, g3_ref, b3_ref, o_ref):
    tm = h_ref.shape[0]
    groups = [slice(r * tm // MIX_ROW_GROUPS, (r + 1) * tm // MIX_ROW_GROUPS)
              for r in range(MIX_ROW_GROUPS)]
    mix = [_dot(oda_ref[r, :], wo1_ref[...]) + _dot(og_ref[r, :], wo2_ref[...]) for r in groups]
    h2 = [_layer_norm(ALPHA * h_ref[r, :] + m, g2_ref[...], b2_ref[...]) for r, m in zip(groups, mix)]
    q = [(_dot(x.astype(BF16), wq_ref[...]) * (CROSS_HEAD_DIM ** -0.5)).astype(BF16) for x in h2]
    heads = [[] for _ in groups]
    for h in range(CROSS_HEADS):
        cols = slice(h * CROSS_HEAD_DIM, (h + 1) * CROSS_HEAD_DIM)
        for g, qg in enumerate(q):
            s = lax.dot_general(qg[:, cols], k_ref[0, :, cols], _NT, preferred_element_type=F32)
            p = jnp.exp(s - jnp.max(s, axis=-1, keepdims=True))
            l = jnp.sum(p, axis=-1, keepdims=True)
            heads[g].append((_dot(p.astype(BF16), v_ref[0, :, cols]) / l).astype(BF16))
    for r, x, hs in zip(groups, h2, heads):
        c = _dot(jnp.concatenate(hs, axis=-1), wo_ref[...])
        o_ref[r, :] = _layer_norm(ALPHA * x + c, g3_ref[...], b3_ref[...])


def _mix_cross(o_da, o_g, h1, wo1, wo2, g2, b2, wq, k_mem, v_mem, wo, g3, b3, seq):
    n = h1.shape[0]
    tm = TOKEN_TILE
    per_b = seq // tm
    row = lambda w: pl.BlockSpec((tm, w), lambda i: (i, 0))
    kv = pl.BlockSpec((1,) + k_mem.shape[1:], lambda i: (i // per_b, 0, 0))
    return pl.pallas_call(
        _mix_cross_kernel,
        grid=(n // tm,),
        in_specs=[row(DA_WIDTH), row(GLA_V_WIDTH), row(D_MODEL),
                  _const_spec(wo1.shape), _const_spec(wo2.shape),
                  _const_spec(g2.shape), _const_spec(b2.shape),
                  _const_spec(wq.shape), kv, kv, _const_spec(wo.shape),
                  _const_spec(g3.shape), _const_spec(b3.shape)],
        out_specs=row(D_MODEL),
        out_shape=jax.ShapeDtypeStruct((n, D_MODEL), F32),
        compiler_params=_params("parallel"),
        name="mix_cross",
    )(o_da, o_g, h1, wo1, wo2, g2, b2, wq, k_mem, v_mem, wo, g3, b3)


def kernel(x, mem, ffn1_w_gate, ffn1_w_up, ffn1_w_down, ln1_g, ln1_b, w_in, da_lambda_q1, da_lambda_k1, da_lambda_q2, da_lambda_k2, da_norm_g, gla_w_gate2, gla_b_gate, gla_norm_g, w_out, ln2_g, ln2_b, cross_wq, cross_wkv, cross_wo, ln3_g, ln3_b, ffn2_w_gate, ffn2_w_up, ffn2_w_down, ln4_g, ln4_b):
    batch, seq, d = x.shape
    assert d == D_MODEL and seq % TOKEN_TILE == 0 and TOKEN_TILE % DA_T == 0
    assert (batch * seq) % FFN_TILE == 0 and FFN_TILE % FFN_ROW_GROUPS == 0
    n = batch * seq
    h = x.reshape(n, d)
    row = lambda a, l: a[l].reshape(1, -1)
    for l in range(DEPTH):
        bf = lambda w: w[l].astype(BF16)
        lambda_init = 0.8 - 0.6 * math.exp(-0.3 * l)
        w_in_l = w_in[l]
        pad = LANES - GLA_GATE_RANK
        w_lr = jnp.pad(w_in_l[:, _OFF_GATE:], ((0, 0), (0, pad))).astype(BF16)
        w_g2 = jnp.pad(gla_w_gate2[l], ((0, pad), (0, 0))).astype(BF16)

        h1 = _ffn_ln(h, ffn1_w_gate[l], ffn1_w_up[l], ffn1_w_down[l], row(ln1_g, l), row(ln1_b, l))
        qt, k_da, vt, gla_main, kend, dec = _in_proj(
            h1, w_in_l[:, _OFF_DA:_OFF_GLA].astype(BF16), w_in_l[:, _OFF_GLA:_OFF_GATE].astype(BF16),
            w_lr, w_g2, row(gla_b_gate, l), batch, seq)
        o_da = _diff_attn(qt, k_da.reshape(batch, seq, DA_WIDTH), vt,
                          row(da_lambda_q1, l), row(da_lambda_k1, l),
                          row(da_lambda_q2, l), row(da_lambda_k2, l),
                          da_norm_g[l].reshape(-1, 1), lambda_init)
        o_g = _gla(gla_main, kend, dec, row(gla_norm_g, l), batch, seq)
        k_mem, v_mem = _mem_kv(mem, bf(cross_wkv))
        w_out_l = bf(w_out)
        h3 = _mix_cross(o_da.reshape(n, DA_WIDTH), o_g, h1, w_out_l[:DA_WIDTH], w_out_l[DA_WIDTH:],
                        row(ln2_g, l), row(ln2_b, l), bf(cross_wq), k_mem, v_mem, bf(cross_wo),
                        row(ln3_g, l), row(ln3_b, l), seq)
        h = _ffn_ln(h3, ffn2_w_gate[l], ffn2_w_up[l], ffn2_w_down[l], row(ln4_g, l), row(ln4_b, l))
    return h.reshape(batch, seq, d)
```

```python
import functools
import math

import jax
import jax.numpy as jnp
from jax import lax
from jax.experimental import pallas as pl
from jax.experimental.pallas import tpu as pltpu

D_MODEL = 1024
CHUNK = 64
DA_HEADS = 4
DA_HEAD_DIM = 64
DA_WIDTH = DA_HEADS * 2 * DA_HEAD_DIM
GLA_HEADS = 4
GLA_DK = 64
GLA_DV = 128
GLA_K_WIDTH = GLA_HEADS * GLA_DK
GLA_V_WIDTH = GLA_HEADS * GLA_DV
GLA_GATE_RANK = 16
GLA_GATE_NORM = 16.0
FFN_HIDDEN = 2816
CROSS_HEADS = 4
CROSS_HEAD_DIM = D_MODEL // CROSS_HEADS
DEPTH = 1
ALPHA = (2.0 * DEPTH) ** 0.25
EPS = 1e-5
LANES = 128

_OFF_DA = 0
_OFF_GLA = 3 * DA_WIDTH
_GLA_MAIN = 2 * GLA_K_WIDTH + 2 * GLA_V_WIDTH
_OFF_GATE = _OFF_GLA + _GLA_MAIN

MXU_DIM = 256
FFN_SPLITS = (0, 1536, FFN_HIDDEN)
assert all(s % MXU_DIM == 0 for s in FFN_SPLITS)
FFN_TILE = 1024
FFN_ROW_GROUPS = 4
FFN_CAST_STEPS = 8
IN_TILE = 1024
GLA_TILE = 1024
MIX_TILE = 1024
MIX_ROW_GROUPS = 4
DA_T = 256
VMEM_LIMIT = 56 * 1024 * 1024

F32 = jnp.float32
BF16 = jnp.bfloat16
_NEG = -1e30
_DA_Q_SCALE = DA_HEAD_DIM ** -0.5 * math.log2(math.e)
_DA_V_PAD = 16
_DA_V_ROWS = 2 * DA_HEAD_DIM + _DA_V_PAD
_DA_ACC_ROWS = 2 * DA_HEAD_DIM + 8
_DA_UNROLL = 34

_NT = (((1,), (1,)), ((), ()))
_TN = (((0,), (0,)), ((), ()))


def _dot(a, b):
    return jnp.dot(a, b, preferred_element_type=F32)


def _split_bf16(a):
    hi = a.astype(BF16)
    lo = (a - hi.astype(F32)).astype(BF16)
    return hi, lo


def _layer_norm(z, g, b):
    mu = jnp.mean(z, axis=-1, keepdims=True)
    zc = z - mu
    var = jnp.mean(zc * zc, axis=-1, keepdims=True)
    return zc * lax.rsqrt(var + EPS) * g + b


def _const_spec(shape):
    nd = len(shape)
    return pl.BlockSpec(shape, lambda *_: (0,) * nd, pipeline_mode=pl.Buffered(1))


def _params(*sem):
    return pltpu.CompilerParams(dimension_semantics=sem, vmem_limit_bytes=VMEM_LIMIT)


def _ffn_ln_kernel(x_ref, wg32_ref, wu32_ref, wd32_ref, g_ref, b_ref, o_ref, wg_ref, wu_ref, wd_ref):
    i = pl.program_id(0)

    @pl.when(i < FFN_CAST_STEPS)
    def _():
        for src, dst in ((wg32_ref, wg_ref), (wu32_ref, wu_ref), (wd32_ref, wd_ref)):
            rows = src.shape[0]
            dst[pl.ds(pl.multiple_of(i * rows, rows), rows), :] = src[...].astype(BF16)

    @pl.when(i >= FFN_CAST_STEPS)
    def _():
        tm = x_ref.shape[0]
        for r in range(FFN_ROW_GROUPS):
            rows = slice(r * tm // FFN_ROW_GROUPS, (r + 1) * tm // FFN_ROW_GROUPS)
            x = x_ref[rows, :]
            xb = x.astype(BF16)
            y = None
            for lo, hi in zip(FFN_SPLITS[:-1], FFN_SPLITS[1:]):
                cols = slice(lo, hi)
                gate = _dot(xb, wg_ref[:, cols])
                up = _dot(xb, wu_ref[:, cols])
                hid = (gate * jax.nn.sigmoid(gate) * up).astype(BF16)
                part = _dot(hid, wd_ref[cols, :])
                y = part if y is None else y + part
            o_ref[rows, :] = _layer_norm(ALPHA * x + 0.5 * y, g_ref[...], b_ref[...])


def _ffn_ln(x, wg, wu, wd, g, b):
    n = x.shape[0]
    tm = FFN_TILE
    nc = FFN_CAST_STEPS
    row = pl.BlockSpec((tm, D_MODEL), lambda i: (jnp.maximum(i - nc, 0), 0))
    w_chunk = lambda w: pl.BlockSpec((w.shape[0] // nc, w.shape[1]), lambda i: (jnp.minimum(i, nc - 1), 0))
    return pl.pallas_call(
        _ffn_ln_kernel,
        grid=(nc + n // tm,),
        in_specs=[row, w_chunk(wg), w_chunk(wu), w_chunk(wd),
                  _const_spec(g.shape), _const_spec(b.shape)],
        out_specs=row,
        out_shape=jax.ShapeDtypeStruct((n, D_MODEL), F32),
        scratch_shapes=[pltpu.VMEM(wg.shape, BF16), pltpu.VMEM(wu.shape, BF16), pltpu.VMEM(wd.shape, BF16)],
        compiler_params=_params("arbitrary"),
        name="ffn_ln",
    )(x, wg, wu, wd, g, b)


def _in_proj_kernel(h_ref, wda_ref, wgla_ref, wlr_ref, wg2_ref, bg_ref, later_ref,
                    qt_ref, k_ref, vt_ref, gla_ref, kend_ref, dec_ref):
    tm = h_ref.shape[0]
    hd2 = 2 * DA_HEAD_DIM
    t = DA_T
    sub = lax.broadcasted_iota(jnp.int32, (_DA_V_PAD, t), 0)
    ones_row = jnp.where(sub == 0, 1.0, 0.0).astype(BF16)

    for j in range(tm // t):
        rows = slice(j * t, (j + 1) * t)
        hb = h_ref[rows, :].astype(BF16)

        g_lr = _dot(hb, wlr_ref[...])
        pre = _dot(g_lr.astype(BF16), wg2_ref[...]) + bg_ref[...]
        log_a = (jnp.minimum(pre, 0.0) - jnp.log(1.0 + jnp.exp(-jnp.abs(pre)))) / GLA_GATE_NORM
        la_hi, la_lo = _split_bf16(log_a)
        rev = _dot(later_ref[...], la_hi) + _dot(later_ref[...], la_lo)
        to_end = jnp.exp(rev)
        total = jnp.sum(log_a.reshape(t // CHUNK, CHUNK, GLA_K_WIDTH), axis=1)
        dec_ref[j * (t // CHUNK):(j + 1) * (t // CHUNK), :] = jnp.exp(total)

        y_g = _dot(hb, wgla_ref[...])
        gla_ref[rows, :] = y_g.astype(BF16)
        kend_ref[rows, :] = (y_g[:, GLA_K_WIDTH:2 * GLA_K_WIDTH] * to_end).astype(BF16)

        y_da = _dot(hb, wda_ref[...])
        k_ref[rows, :] = y_da[:, DA_WIDTH:2 * DA_WIDTH].astype(BF16)
        q_t = (y_da[:, :DA_WIDTH] * _DA_Q_SCALE).T
        v_t = y_da[:, 2 * DA_WIDTH:].T
        for h in range(DA_HEADS):
            head = slice(h * hd2, (h + 1) * hd2)
            qt_ref[0, h, j] = q_t[head, :].astype(BF16)
            vt_ref[0, h, j, :hd2, :] = v_t[head, :].astype(BF16)
            vt_ref[0, h, j, hd2:, :] = ones_row


def _in_proj(h, w_da, w_gla, w_lr, w_g2, b_g, batch, seq):
    n = h.shape[0]
    tm = IN_TILE
    per_b = seq // tm
    row = lambda w: pl.BlockSpec((tm, w), lambda i: (i, 0))
    blocked = lambda i: (i // per_b, 0, i % per_b, 0, 0)
    step = jnp.arange(DA_T)
    later = ((step[None, :] > step[:, None])
             & (step[None, :] // CHUNK == step[:, None] // CHUNK)).astype(BF16)
    return pl.pallas_call(
        _in_proj_kernel,
        grid=(n // tm,),
        in_specs=[row(D_MODEL), _const_spec(w_da.shape), _const_spec(w_gla.shape),
                  _const_spec(w_lr.shape), _const_spec(w_g2.shape), _const_spec(b_g.shape),
                  _const_spec(later.shape)],
        out_specs=[
            pl.BlockSpec((1, DA_HEADS, tm // DA_T, 2 * DA_HEAD_DIM, DA_T), blocked),
            row(DA_WIDTH),
            pl.BlockSpec((1, DA_HEADS, tm // DA_T, _DA_V_ROWS, DA_T), blocked),
            row(_GLA_MAIN),
            row(GLA_K_WIDTH),
            pl.BlockSpec((tm // CHUNK, GLA_K_WIDTH), lambda i: (i, 0)),
        ],
        out_shape=[
            jax.ShapeDtypeStruct((batch, DA_HEADS, seq // DA_T, 2 * DA_HEAD_DIM, DA_T), BF16),
            jax.ShapeDtypeStruct((n, DA_WIDTH), BF16),
            jax.ShapeDtypeStruct((batch, DA_HEADS, seq // DA_T, _DA_V_ROWS, DA_T), BF16),
            jax.ShapeDtypeStruct((n, _GLA_MAIN), BF16),
            jax.ShapeDtypeStruct((n, GLA_K_WIDTH), BF16),
            jax.ShapeDtypeStruct((n // CHUNK, GLA_K_WIDTH), F32),
        ],
        compiler_params=_params("parallel"),
        name="in_proj",
    )(h, w_da, w_gla, w_lr, w_g2, b_g, later)


def _diff_attn_kernel(it_ref, jt_ref, lq1_ref, lk1_ref, lq2_ref, lk2_ref, g_ref, e_ref, mk_ref,
                      qt_ref, k_ref, vt_ref, o_ref, s_a, s_b, p_a, p_b, acc_ref,
                      *, lambda_init, n_items):
    t = DA_T
    hd2 = 2 * DA_HEAD_DIM
    s_bufs = (s_a, s_b)
    p_bufs = (p_a, p_b)
    row = lax.broadcasted_iota(jnp.int32, (hd2, t), 0)
    halves = (row < DA_HEAD_DIM, row >= DA_HEAD_DIM)

    def scores(x, s_ref):
        i, j = it_ref[x], jt_ref[x]
        kj = k_ref[0, pl.ds(pl.multiple_of(j * t, t), t), :]
        k_ext = jnp.concatenate([kj, e_ref[...]], axis=1)
        qt = qt_ref[0, 0, i]
        diag = jnp.where(i == j, 1.0, 0.0)
        mk = (mk_ref[...] * diag).astype(BF16)
        pad = jnp.zeros((hd2 - mk.shape[0], t), BF16)
        top = []
        for c in range(2):
            rhs = jnp.concatenate([jnp.where(halves[c], qt, 0), mk, pad], axis=0)
            s = _dot(k_ext, rhs)
            s_ref[c] = s
            top.append(jnp.max(s, axis=0, keepdims=True))
        return tuple(top)

    def softmax(x, s_ref, p_ref, m, top):
        first = jt_ref[x] == 0
        m_out, a_out = [], []
        for c in range(2):
            m_old = jnp.where(first, _NEG, m[c])
            m_new = jnp.maximum(m_old, top[c])
            p_ref[c] = jnp.exp2(s_ref[c] - m_new).astype(BF16)
            m_out.append(m_new)
            a_out.append(jnp.exp2(m_old - m_new))
        return tuple(m_out), tuple(a_out)

    def values(x, p_ref, a):
        i, j = it_ref[x], jt_ref[x]
        vtj = vt_ref[0, 0, j]
        for c in range(2):
            acc_ref[i, c] = a[c] * acc_ref[i, c] + _dot(vtj, p_ref[c])[:_DA_ACC_ROWS]

    acc_ref[...] = jnp.zeros_like(acc_ref)
    p_b[...] = jnp.zeros_like(p_b)
    top0 = scores(0, s_a)

    def body(step, carry):
        m, a_prev, top = carry
        for u in range(_DA_UNROLL):
            x = step * _DA_UNROLL + u
            top_next = scores(x + 1, s_bufs[(u + 1) % 2])
            m, a = softmax(x, s_bufs[u % 2], p_bufs[u % 2], m, top)
            values(jnp.maximum(x - 1, 0), p_bufs[(u + 1) % 2], a_prev)
            a_prev, top = a, top_next
        return m, a_prev, top

    neg = jnp.full((1, t), _NEG, F32)
    one = jnp.ones((1, t), F32)
    _, a_last, _ = lax.fori_loop(0, n_items // _DA_UNROLL, body, ((neg, neg), (one, one), top0))
    values(n_items - 1, p_bufs[(n_items - 1) % 2], a_last)

    lam = (jnp.exp(jnp.sum(lq1_ref[...] * lk1_ref[...], axis=-1, keepdims=True))
           - jnp.exp(jnp.sum(lq2_ref[...] * lk2_ref[...], axis=-1, keepdims=True))
           + lambda_init)

    def finish(i, _):
        w0 = 1.0 / acc_ref[i, 0, hd2:hd2 + 1, :]
        w1 = lam / acc_ref[i, 1, hd2:hd2 + 1, :]
        o = acc_ref[i, 0, :hd2, :] * w0 - acc_ref[i, 1, :hd2, :] * w1
        ms = jnp.mean(o * o, axis=0, keepdims=True)
        y = o * (lax.rsqrt(ms + EPS) * (1.0 - lambda_init)) * g_ref[...]
        o_ref[0, pl.ds(pl.multiple_of(i * t, t), t), :] = y.astype(BF16).T
        return 0

    lax.fori_loop(0, acc_ref.shape[0], finish, 0, unroll=2)


def _diff_attn(qt, k, vt, lq1, lk1, lq2, lk2, g_col, lambda_init):
    batch, seq, _ = k.shape
    hd2 = 2 * DA_HEAD_DIM
    t = DA_T
    nq = seq // t
    items = [(i, j) for i in range(nq) for j in range(i + 1)]
    n_items = len(items)
    assert n_items % _DA_UNROLL == 0 and _DA_UNROLL % 2 == 0 and t % CHUNK == 0
    items.append(items[-1])
    i_tab = jnp.asarray([i for i, _ in items], jnp.int32)
    j_tab = jnp.asarray([j for _, j in items], jnp.int32)
    chunk = jnp.arange(t) // CHUNK
    e = (chunk[:, None] == jnp.arange(hd2)[None, :]).astype(BF16)
    ids = jnp.arange(_DA_V_PAD)
    mk = jnp.where((ids[:, None] < t // CHUNK) & (chunk[None, :] < ids[:, None]), _NEG, 0.0).astype(F32)

    const = lambda shape: pl.BlockSpec(shape, lambda *_: (0,) * len(shape), pipeline_mode=pl.Buffered(1))
    head = lambda rows: pl.BlockSpec((1, 1, nq, rows, t), lambda b, h, *_: (b, h, 0, 0, 0))
    return pl.pallas_call(
        functools.partial(_diff_attn_kernel, lambda_init=lambda_init, n_items=n_items),
        grid_spec=pltpu.PrefetchScalarGridSpec(
            num_scalar_prefetch=2,
            grid=(batch, DA_HEADS),
            in_specs=[const(lq1.shape), const(lk1.shape), const(lq2.shape), const(lk2.shape),
                      const(g_col.shape), const(e.shape), const(mk.shape),
                      head(hd2),
                      pl.BlockSpec((1, seq, hd2), lambda b, h, *_: (b, 0, h)),
                      head(_DA_V_ROWS)],
            out_specs=pl.BlockSpec((1, seq, hd2), lambda b, h, *_: (b, 0, h)),
            scratch_shapes=[pltpu.VMEM((2, t, t), F32), pltpu.VMEM((2, t, t), F32),
                            pltpu.VMEM((2, t, t), BF16), pltpu.VMEM((2, t, t), BF16),
                            pltpu.VMEM((nq, 2, _DA_ACC_ROWS, t), F32)],
        ),
        out_shape=jax.ShapeDtypeStruct((batch, seq, DA_WIDTH), BF16),
        compiler_params=_params("parallel", "parallel"),
        name="diff_attn",
    )(i_tab, j_tab, lq1, lk1, lq2, lk2, g_col, e, mk, qt, k, vt)


def _gla_kernel(g_ref, main_ref, kend_ref, dec_ref, o_ref, st_ref, stb_ref, upd_ref, gate_ref):
    @pl.when(pl.program_id(1) == 0)
    def _():
        st_ref[...] = jnp.zeros_like(st_ref)
        stb_ref[...] = jnp.zeros_like(stb_ref)

    tm = main_ref.shape[0]
    chunks = [slice(c * CHUNK, (c + 1) * CHUNK) for c in range(tm // CHUNK)]
    blocks = [(slice(h * GLA_DV, (h + 1) * GLA_DV), slice(h * GLA_DK, (h + 1) * GLA_DK))
              for h in range(GLA_HEADS)]

    for c, rows in enumerate(chunks):
        r = main_ref[rows, 2 * GLA_K_WIDTH + GLA_V_WIDTH:].astype(F32)
        gate_ref[rows, :] = r * jax.nn.sigmoid(r)
        v = main_ref[rows, 2 * GLA_K_WIDTH:2 * GLA_K_WIDTH + GLA_V_WIDTH]
        upd = lax.dot_general(v, kend_ref[rows, :], _TN, preferred_element_type=F32)
        for vr, kc in blocks:
            upd_ref[c, vr, kc] = upd[vr, kc]
    for c in range(len(chunks)):
        dec = dec_ref[c:c + 1, :]
        for vr, kc in blocks:
            new = st_ref[vr, kc] * dec[:, kc] + upd_ref[c, vr, kc]
            st_ref[vr, kc] = new
            stb_ref[c, vr, kc] = new.astype(BF16)
    for c, rows in enumerate(chunks):
        q = main_ref[rows, :GLA_K_WIDTH] * (GLA_DK ** -0.5)
        out = lax.dot_general(q.astype(BF16), stb_ref[c], _NT,
                              preferred_element_type=F32)
        for vr, _ in blocks:
            o = out[:, vr]
            ms = jnp.mean(o * o, axis=-1, keepdims=True)
            o_ref[rows, vr] = (o * lax.rsqrt(ms + EPS) * g_ref[...] * gate_ref[rows, vr]).astype(BF16)


def _gla(main, kend, dec, g_row, batch, seq):
    n = main.shape[0]
    tm = GLA_TILE
    per_b = seq // tm
    idx = lambda b, i: (b * per_b + i, 0)
    return pl.pallas_call(
        _gla_kernel,
        grid=(batch, per_b),
        in_specs=[_const_spec(g_row.shape),
                  pl.BlockSpec((tm, _GLA_MAIN), idx),
                  pl.BlockSpec((tm, GLA_K_WIDTH), idx),
                  pl.BlockSpec((tm // CHUNK, GLA_K_WIDTH), idx)],
        out_specs=pl.BlockSpec((tm, GLA_V_WIDTH), idx),
        out_shape=jax.ShapeDtypeStruct((n, GLA_V_WIDTH), BF16),
        scratch_shapes=[pltpu.VMEM((GLA_V_WIDTH, GLA_K_WIDTH), F32),
                        pltpu.VMEM((tm // CHUNK, GLA_V_WIDTH, GLA_K_WIDTH), BF16),
                        pltpu.VMEM((tm // CHUNK, GLA_V_WIDTH, GLA_K_WIDTH), F32),
                        pltpu.VMEM((tm, GLA_V_WIDTH), F32)],
        compiler_params=_params("parallel", "arbitrary"),
        name="gla",
    )(g_row, main, kend, dec)


def _mem_kv_kernel(m_ref, w_ref, k_ref, v_ref):
    y = _dot(m_ref[0].astype(BF16), w_ref[...])
    k_ref[0] = y[:, :D_MODEL].astype(BF16)
    v_ref[0] = y[:, D_MODEL:].astype(BF16)


def _mem_kv(mem, wkv):
    batch, m, _ = mem.shape
    blk = pl.BlockSpec((1, m, D_MODEL), lambda b: (b, 0, 0))
    out = jax.ShapeDtypeStruct((batch, m, D_MODEL), BF16)
    return pl.pallas_call(
        _mem_kv_kernel,
        grid=(batch,),
        in_specs=[blk, _const_spec(wkv.shape)],
        out_specs=[blk, blk],
        out_shape=[out, out],
        compiler_params=_params("parallel"),
        name="mem_kv",
    )(mem, wkv)


def _mix_cross_kernel(oda_ref, og_ref, h_ref, wo1_ref, wo2_ref, g2_ref, b2_ref,
                      wq_ref, k_ref, v_ref, wo_ref, g3_ref, b3_ref, o_ref):
    tm = h_ref.shape[0]
    groups = [slice(r * tm // MIX_ROW_GROUPS, (r + 1) * tm // MIX_ROW_GROUPS)
              for r in range(MIX_ROW_GROUPS)]
    mix = [_dot(oda_ref[r, :], wo1_ref[...]) + _dot(og_ref[r, :], wo2_ref[...]) for r in groups]
    h2 = [_layer_norm(ALPHA * h_ref[r, :] + m, g2_ref[...], b2_ref[...]) for r, m in zip(groups, mix)]
    q = [(_dot(x.astype(BF16), wq_ref[...]) * (CROSS_HEAD_DIM ** -0.5)).astype(BF16) for x in h2]
    heads = [[] for _ in groups]
    for h in range(CROSS_HEADS):
        cols = slice(h * CROSS_HEAD_DIM, (h + 1) * CROSS_HEAD_DIM)
        for g, qg in enumerate(q):
            s = lax.dot_general(qg[:, cols], k_ref[0, :, cols], _NT, preferred_element_type=F32)
            p = jnp.exp(s - jnp.max(s, axis=-1, keepdims=True))
            l = jnp.sum(p, axis=-1, keepdims=True)
            heads[g].append((_dot(p.astype(BF16), v_ref[0, :, cols]) / l).astype(BF16))
    for r, x, hs in zip(groups, h2, heads):
        c = _dot(jnp.concatenate(hs, axis=-1), wo_ref[...])
        o_ref[r, :] = _layer_norm(ALPHA * x + c, g3_ref[...], b3_ref[...])


def _mix_cross(o_da, o_g, h1, wo1, wo2, g2, b2, wq, k_mem, v_mem, wo, g3, b3, seq):
    n = h1.shape[0]
    tm = MIX_TILE
    per_b = seq // tm
    row = lambda w: pl.BlockSpec((tm, w), lambda i: (i, 0))
    kv = pl.BlockSpec((1,) + k_mem.shape[1:], lambda i: (i // per_b, 0, 0))
    return pl.pallas_call(
        _mix_cross_kernel,
        grid=(n // tm,),
        in_specs=[row(DA_WIDTH), row(GLA_V_WIDTH), row(D_MODEL),
                  _const_spec(wo1.shape), _const_spec(wo2.shape),
                  _const_spec(g2.shape), _const_spec(b2.shape),
                  _const_spec(wq.shape), kv, kv, _const_spec(wo.shape),
                  _const_spec(g3.shape), _const_spec(b3.shape)],
        out_specs=row(D_MODEL),
        out_shape=jax.ShapeDtypeStruct((n, D_MODEL), F32),
        compiler_params=_params("parallel"),
        name="mix_cross",
    )(o_da, o_g, h1, wo1, wo2, g2, b2, wq, k_mem, v_mem, wo, g3, b3)


def kernel(x, mem, ffn1_w_gate, ffn1_w_up, ffn1_w_down, ln1_g, ln1_b, w_in, da_lambda_q1, da_lambda_k1, da_lambda_q2, da_lambda_k2, da_norm_g, gla_w_gate2, gla_b_gate, gla_norm_g, w_out, ln2_g, ln2_b, cross_wq, cross_wkv, cross_wo, ln3_g, ln3_b, ffn2_w_gate, ffn2_w_up, ffn2_w_down, ln4_g, ln4_b):
    batch, seq, d = x.shape
    assert d == D_MODEL and IN_TILE % DA_T == 0 and GLA_TILE % CHUNK == 0
    assert all(seq % tile == 0 for tile in (IN_TILE, GLA_TILE, MIX_TILE))
    assert (batch * seq) % FFN_TILE == 0 and FFN_TILE % FFN_ROW_GROUPS == 0
    n = batch * seq
    h = x.reshape(n, d)
    row = lambda a, l: a[l].reshape(1, -1)
    for l in range(DEPTH):
        bf = lambda w: w[l].astype(BF16)
        lambda_init = 0.8 - 0.6 * math.exp(-0.3 * l)
        w_in_l = w_in[l]
        pad = LANES - GLA_GATE_RANK
        w_lr = jnp.pad(w_in_l[:, _OFF_GATE:], ((0, 0), (0, pad))).astype(BF16)
        w_g2 = jnp.pad(gla_w_gate2[l], ((0, pad), (0, 0))).astype(BF16)

        h1 = _ffn_ln(h, ffn1_w_gate[l], ffn1_w_up[l], ffn1_w_down[l], row(ln1_g, l), row(ln1_b, l))
        qt, k_da, vt, gla_main, kend, dec = _in_proj(
            h1, w_in_l[:, _OFF_DA:_OFF_GLA].astype(BF16), w_in_l[:, _OFF_GLA:_OFF_GATE].astype(BF16),
            w_lr, w_g2, row(gla_b_gate, l), batch, seq)
        o_da = _diff_attn(qt, k_da.reshape(batch, seq, DA_WIDTH), vt,
                          row(da_lambda_q1, l), row(da_lambda_k1, l),
                          row(da_lambda_q2, l), row(da_lambda_k2, l),
                          da_norm_g[l].reshape(-1, 1), lambda_init)
        o_g = _gla(gla_main, kend, dec, row(gla_norm_g, l), batch, seq)
        k_mem, v_mem = _mem_kv(mem, bf(cross_wkv))
        w_out_l = bf(w_out)
        h3 = _mix_cross(o_da.reshape(n, DA_WIDTH), o_g, h1, w_out_l[:DA_WIDTH], w_out_l[DA_WIDTH:],
                        row(ln2_g, l), row(ln2_b, l), bf(cross_wq), k_mem, v_mem, bf(cross_wo),
                        row(ln3_g, l), row(ln3_b, l), seq)
        h = _ffn_ln(h3, ffn2_w_gate[l], ffn2_w_up[l], ffn2_w_down[l], row(ln4_g, l), row(ln4_b, l))
    return h.reshape(batch, seq, d)
```

```python
import functools
import math

import jax
import jax.numpy as jnp
from jax import lax
from jax.experimental import pallas as pl
from jax.experimental.pallas import tpu as pltpu

D_MODEL = 1024
CHUNK = 64
DA_HEADS = 4
DA_HEAD_DIM = 64
DA_WIDTH = DA_HEADS * 2 * DA_HEAD_DIM
GLA_HEADS = 4
GLA_DK = 64
GLA_DV = 128
GLA_K_WIDTH = GLA_HEADS * GLA_DK
GLA_V_WIDTH = GLA_HEADS * GLA_DV
GLA_GATE_RANK = 16
GLA_GATE_NORM = 16.0
FFN_HIDDEN = 2816
CROSS_HEADS = 4
CROSS_HEAD_DIM = D_MODEL // CROSS_HEADS
DEPTH = 1
ALPHA = (2.0 * DEPTH) ** 0.25
EPS = 1e-5
LANES = 128

_OFF_DA = 0
_OFF_GLA = 3 * DA_WIDTH
_GLA_MAIN = 2 * GLA_K_WIDTH + 2 * GLA_V_WIDTH
_OFF_GATE = _OFF_GLA + _GLA_MAIN

MXU_DIM = 256
FFN_SPLITS = (0, FFN_HIDDEN)
assert all(s % MXU_DIM == 0 for s in FFN_SPLITS)
FFN_TILE = 1024
FFN_ROW_GROUPS = 4
FFN_CAST_STEPS = 8
IN_TILE = 1024
MIX_TILE = 1024
MIX_ROW_GROUPS = 4
DA_T = 256
VMEM_LIMIT = 56 * 1024 * 1024

F32 = jnp.float32
BF16 = jnp.bfloat16
_NEG = -1e30
_DA_Q_SCALE = DA_HEAD_DIM ** -0.5 * math.log2(math.e)
_DA_V_PAD = 16
_DA_V_ROWS = 2 * DA_HEAD_DIM + _DA_V_PAD
_DA_ACC_ROWS = 2 * DA_HEAD_DIM + 8
_DA_UNROLL = 34

_NT = (((1,), (1,)), ((), ()))
_TN = (((0,), (0,)), ((), ()))


def _dot(a, b):
    return jnp.dot(a, b, preferred_element_type=F32)


def _split_bf16(a):
    hi = a.astype(BF16)
    lo = (a - hi.astype(F32)).astype(BF16)
    return hi, lo


def _layer_norm(z, g, b):
    mu = jnp.mean(z, axis=-1, keepdims=True)
    zc = z - mu
    var = jnp.mean(zc * zc, axis=-1, keepdims=True)
    return zc * lax.rsqrt(var + EPS) * g + b


def _const_spec(shape):
    nd = len(shape)
    return pl.BlockSpec(shape, lambda *_: (0,) * nd, pipeline_mode=pl.Buffered(1))


def _params(*sem):
    return pltpu.CompilerParams(dimension_semantics=sem, vmem_limit_bytes=VMEM_LIMIT)


def _ffn_ln_kernel(x_ref, wg32_ref, wu32_ref, wd32_ref, g_ref, b_ref, o_ref, wg_ref, wu_ref, wd_ref):
    i = pl.program_id(0)

    @pl.when(i < FFN_CAST_STEPS)
    def _():
        for src, dst in ((wg32_ref, wg_ref), (wu32_ref, wu_ref), (wd32_ref, wd_ref)):
            rows = src.shape[0]
            dst[pl.ds(pl.multiple_of(i * rows, rows), rows), :] = src[...].astype(BF16)

    @pl.when(i >= FFN_CAST_STEPS)
    def _():
        tm = x_ref.shape[0]
        for r in range(FFN_ROW_GROUPS):
            rows = slice(r * tm // FFN_ROW_GROUPS, (r + 1) * tm // FFN_ROW_GROUPS)
            x = x_ref[rows, :]
            xb = x.astype(BF16)
            y = None
            for lo, hi in zip(FFN_SPLITS[:-1], FFN_SPLITS[1:]):
                cols = slice(lo, hi)
                gate = _dot(xb, wg_ref[:, cols])
                up = _dot(xb, wu_ref[:, cols])
                hid = (gate * jax.nn.sigmoid(gate) * up).astype(BF16)
                part = _dot(hid, wd_ref[cols, :])
                y = part if y is None else y + part
            o_ref[rows, :] = _layer_norm(ALPHA * x + 0.5 * y, g_ref[...], b_ref[...])


def _ffn_ln(x, wg, wu, wd, g, b):
    n = x.shape[0]
    tm = FFN_TILE
    nc = FFN_CAST_STEPS
    row = pl.BlockSpec((tm, D_MODEL), lambda i: (jnp.maximum(i - nc, 0), 0))
    w_chunk = lambda w: pl.BlockSpec((w.shape[0] // nc, w.shape[1]), lambda i: (jnp.minimum(i, nc - 1), 0))
    return pl.pallas_call(
        _ffn_ln_kernel,
        grid=(nc + n // tm,),
        in_specs=[row, w_chunk(wg), w_chunk(wu), w_chunk(wd),
                  _const_spec(g.shape), _const_spec(b.shape)],
        out_specs=row,
        out_shape=jax.ShapeDtypeStruct((n, D_MODEL), F32),
        scratch_shapes=[pltpu.VMEM(wg.shape, BF16), pltpu.VMEM(wu.shape, BF16), pltpu.VMEM(wd.shape, BF16)],
        compiler_params=_params("arbitrary"),
        name="ffn_ln",
    )(x, wg, wu, wd, g, b)


def _in_proj_kernel(h_ref, wda_ref, wgla_ref, wlr_ref, wg2_ref, bg_ref, later_ref, gn_ref,
                    qt_ref, k_ref, vt_ref, og_ref, st_ref, stb_ref, upd_ref, *, tiles_per_batch):
    tm = h_ref.shape[0]
    hd2 = 2 * DA_HEAD_DIM
    t = DA_T
    per_group = t // CHUNK
    sub = lax.broadcasted_iota(jnp.int32, (_DA_V_PAD, t), 0)
    ones_row = jnp.where(sub == 0, 1.0, 0.0).astype(BF16)
    heads_v = [slice(h * GLA_DV, (h + 1) * GLA_DV) for h in range(GLA_HEADS)]
    stacked = (GLA_HEADS * CHUNK, GLA_K_WIDTH)
    own_head = (lax.broadcasted_iota(jnp.int32, stacked, 0) // CHUNK
                == lax.broadcasted_iota(jnp.int32, stacked, 1) // GLA_DK)

    @pl.when(pl.program_id(0) % tiles_per_batch == 0)
    def _():
        st_ref[...] = jnp.zeros_like(st_ref)

    for j in range(tm // t):
        rows = slice(j * t, (j + 1) * t)
        hb = h_ref[rows, :].astype(BF16)

        g_lr = _dot(hb, wlr_ref[...])
        pre = _dot(g_lr.astype(BF16), wg2_ref[...]) + bg_ref[...]
        log_a = (jnp.minimum(pre, 0.0) - jnp.log(1.0 + jnp.exp(-jnp.abs(pre)))) / GLA_GATE_NORM
        la_hi, la_lo = _split_bf16(log_a)
        rev = _dot(later_ref[...], la_hi) + _dot(later_ref[...], la_lo)
        to_end = jnp.exp(rev)
        total = jnp.sum(log_a.reshape(per_group, CHUNK, GLA_K_WIDTH), axis=1)
        decay = jnp.exp(total)

        y_g = _dot(hb, wgla_ref[...])
        q_g = (y_g[:, :GLA_K_WIDTH] * (GLA_DK ** -0.5)).astype(BF16)
        k_end = (y_g[:, GLA_K_WIDTH:2 * GLA_K_WIDTH] * to_end).astype(BF16)
        v_g = y_g[:, 2 * GLA_K_WIDTH:2 * GLA_K_WIDTH + GLA_V_WIDTH].astype(BF16)
        r_g = y_g[:, 2 * GLA_K_WIDTH + GLA_V_WIDTH:]
        gate = r_g * jax.nn.sigmoid(r_g)

        for cc in range(per_group):
            c, cr = j * per_group + cc, slice(cc * CHUNK, (cc + 1) * CHUNK)
            v_rows = jnp.concatenate([v_g[cr, vr] for vr in heads_v], axis=0)
            k_rows = jnp.where(own_head, jnp.concatenate([k_end[cr]] * GLA_HEADS, axis=0), 0)
            upd_ref[c] = lax.dot_general(v_rows, k_rows, _TN, preferred_element_type=F32)
        for cc in range(per_group):
            c = j * per_group + cc
            new = st_ref[...] * decay[cc:cc + 1, :] + upd_ref[c]
            st_ref[...] = new
            stb_ref[c] = new.astype(BF16)
        for cc in range(per_group):
            c, cr = j * per_group + cc, slice(cc * CHUNK, (cc + 1) * CHUNK)
            q_rows = jnp.where(own_head, jnp.concatenate([q_g[cr]] * GLA_HEADS, axis=0), 0)
            out = lax.dot_general(q_rows, stb_ref[c], _NT, preferred_element_type=F32)
            for h, vr in enumerate(heads_v):
                o = out[h * CHUNK:(h + 1) * CHUNK, :]
                ms = jnp.mean(o * o, axis=-1, keepdims=True)
                og_ref[j * t + cc * CHUNK:j * t + (cc + 1) * CHUNK, vr] = (
                    o * lax.rsqrt(ms + EPS) * gn_ref[...] * gate[cr, vr]).astype(BF16)

        y_da = _dot(hb, wda_ref[...])
        k_ref[rows, :] = y_da[:, DA_WIDTH:2 * DA_WIDTH].astype(BF16)
        q_t = (y_da[:, :DA_WIDTH] * _DA_Q_SCALE).T
        v_t = y_da[:, 2 * DA_WIDTH:].T
        for h in range(DA_HEADS):
            head = slice(h * hd2, (h + 1) * hd2)
            qt_ref[0, h, j] = q_t[head, :].astype(BF16)
            vt_ref[0, h, j, :hd2, :] = v_t[head, :].astype(BF16)
            vt_ref[0, h, j, hd2:, :] = ones_row


def _in_proj(h, w_da, w_gla, w_lr, w_g2, b_g, g_norm, batch, seq):
    n = h.shape[0]
    tm = IN_TILE
    per_b = seq // tm
    row = lambda w: pl.BlockSpec((tm, w), lambda i: (i, 0))
    blocked = lambda i: (i // per_b, 0, i % per_b, 0, 0)
    step = jnp.arange(DA_T)
    later = ((step[None, :] > step[:, None])
             & (step[None, :] // CHUNK == step[:, None] // CHUNK)).astype(BF16)
    state = (GLA_DV, GLA_K_WIDTH)
    return pl.pallas_call(
        functools.partial(_in_proj_kernel, tiles_per_batch=per_b),
        grid=(n // tm,),
        in_specs=[row(D_MODEL), _const_spec(w_da.shape), _const_spec(w_gla.shape),
                  _const_spec(w_lr.shape), _const_spec(w_g2.shape), _const_spec(b_g.shape),
                  _const_spec(later.shape), _const_spec(g_norm.shape)],
        out_specs=[
            pl.BlockSpec((1, DA_HEADS, tm // DA_T, 2 * DA_HEAD_DIM, DA_T), blocked),
            row(DA_WIDTH),
            pl.BlockSpec((1, DA_HEADS, tm // DA_T, _DA_V_ROWS, DA_T), blocked),
            row(GLA_V_WIDTH),
        ],
        out_shape=[
            jax.ShapeDtypeStruct((batch, DA_HEADS, seq // DA_T, 2 * DA_HEAD_DIM, DA_T), BF16),
            jax.ShapeDtypeStruct((n, DA_WIDTH), BF16),
            jax.ShapeDtypeStruct((batch, DA_HEADS, seq // DA_T, _DA_V_ROWS, DA_T), BF16),
            jax.ShapeDtypeStruct((n, GLA_V_WIDTH), BF16),
        ],
        scratch_shapes=[pltpu.VMEM(state, F32),
                        pltpu.VMEM((tm // CHUNK,) + state, BF16),
                        pltpu.VMEM((tm // CHUNK,) + state, F32)],
        compiler_params=_params("arbitrary"),
        name="in_proj",
    )(h, w_da, w_gla, w_lr, w_g2, b_g, later, g_norm)


def _diff_attn_kernel(it_ref, jt_ref, lq1_ref, lk1_ref, lq2_ref, lk2_ref, g_ref, e_ref, mk_ref,
                      qt_ref, k_ref, vt_ref, o_ref, s_a, s_b, p_a, p_b, acc_ref,
                      *, lambda_init, n_items):
    t = DA_T
    hd2 = 2 * DA_HEAD_DIM
    s_bufs = (s_a, s_b)
    p_bufs = (p_a, p_b)
    row = lax.broadcasted_iota(jnp.int32, (hd2, t), 0)
    halves = (row < DA_HEAD_DIM, row >= DA_HEAD_DIM)

    def scores(x, s_ref):
        i, j = it_ref[x], jt_ref[x]
        kj = k_ref[0, pl.ds(pl.multiple_of(j * t, t), t), :]
        k_ext = jnp.concatenate([kj, e_ref[...]], axis=1)
        qt = qt_ref[0, 0, i]
        diag = jnp.where(i == j, 1.0, 0.0)
        mk = (mk_ref[...] * diag).astype(BF16)
        pad = jnp.zeros((hd2 - mk.shape[0], t), BF16)
        top = []
        for c in range(2):
            rhs = jnp.concatenate([jnp.where(halves[c], qt, 0), mk, pad], axis=0)
            s = _dot(k_ext, rhs)
            s_ref[c] = s
            top.append(jnp.max(s, axis=0, keepdims=True))
        return tuple(top)

    def softmax(x, s_ref, p_ref, m, top):
        first = jt_ref[x] == 0
        m_out, a_out = [], []
        for c in range(2):
            m_old = jnp.where(first, _NEG, m[c])
            m_new = jnp.maximum(m_old, top[c])
            p_ref[c] = jnp.exp2(s_ref[c] - m_new).astype(BF16)
            m_out.append(m_new)
            a_out.append(jnp.exp2(m_old - m_new))
        return tuple(m_out), tuple(a_out)

    def values(x, p_ref, a):
        i, j = it_ref[x], jt_ref[x]
        vtj = vt_ref[0, 0, j]
        for c in range(2):
            acc_ref[i, c] = a[c] * acc_ref[i, c] + _dot(vtj, p_ref[c])[:_DA_ACC_ROWS]

    acc_ref[...] = jnp.zeros_like(acc_ref)
    p_b[...] = jnp.zeros_like(p_b)
    top0 = scores(0, s_a)

    def body(step, carry):
        m, a_prev, top = carry
        for u in range(_DA_UNROLL):
            x = step * _DA_UNROLL + u
            top_next = scores(x + 1, s_bufs[(u + 1) % 2])
            m, a = softmax(x, s_bufs[u % 2], p_bufs[u % 2], m, top)
            values(jnp.maximum(x - 1, 0), p_bufs[(u + 1) % 2], a_prev)
            a_prev, top = a, top_next
        return m, a_prev, top

    neg = jnp.full((1, t), _NEG, F32)
    one = jnp.ones((1, t), F32)
    _, a_last, _ = lax.fori_loop(0, n_items // _DA_UNROLL, body, ((neg, neg), (one, one), top0))
    values(n_items - 1, p_bufs[(n_items - 1) % 2], a_last)

    lam = (jnp.exp(jnp.sum(lq1_ref[...] * lk1_ref[...], axis=-1, keepdims=True))
           - jnp.exp(jnp.sum(lq2_ref[...] * lk2_ref[...], axis=-1, keepdims=True))
           + lambda_init)

    def finish(i, _):
        w0 = 1.0 / acc_ref[i, 0, hd2:hd2 + 1, :]
        w1 = lam / acc_ref[i, 1, hd2:hd2 + 1, :]
        o = acc_ref[i, 0, :hd2, :] * w0 - acc_ref[i, 1, :hd2, :] * w1
        ms = jnp.mean(o * o, axis=0, keepdims=True)
        y = o * (lax.rsqrt(ms + EPS) * (1.0 - lambda_init)) * g_ref[...]
        o_ref[0, pl.ds(pl.multiple_of(i * t, t), t), :] = y.astype(BF16).T
        return 0

    lax.fori_loop(0, acc_ref.shape[0], finish, 0, unroll=2)


def _diff_attn(qt, k, vt, lq1, lk1, lq2, lk2, g_col, lambda_init):
    batch, seq, _ = k.shape
    hd2 = 2 * DA_HEAD_DIM
    t = DA_T
    nq = seq // t
    items = [(i, j) for i in range(nq) for j in range(i + 1)]
    n_items = len(items)
    assert n_items % _DA_UNROLL == 0 and _DA_UNROLL % 2 == 0 and t % CHUNK == 0
    items.append(items[-1])
    i_tab = jnp.asarray([i for i, _ in items], jnp.int32)
    j_tab = jnp.asarray([j for _, j in items], jnp.int32)
    chunk = jnp.arange(t) // CHUNK
    e = (chunk[:, None] == jnp.arange(hd2)[None, :]).astype(BF16)
    ids = jnp.arange(_DA_V_PAD)
    mk = jnp.where((ids[:, None] < t // CHUNK) & (chunk[None, :] < ids[:, None]), _NEG, 0.0).astype(F32)

    const = lambda shape: pl.BlockSpec(shape, lambda *_: (0,) * len(shape), pipeline_mode=pl.Buffered(1))
    head = lambda rows: pl.BlockSpec((1, 1, nq, rows, t), lambda b, h, *_: (b, h, 0, 0, 0))
    return pl.pallas_call(
        functools.partial(_diff_attn_kernel, lambda_init=lambda_init, n_items=n_items),
        grid_spec=pltpu.PrefetchScalarGridSpec(
            num_scalar_prefetch=2,
            grid=(batch, DA_HEADS),
            in_specs=[const(lq1.shape), const(lk1.shape), const(lq2.shape), const(lk2.shape),
                      const(g_col.shape), const(e.shape), const(mk.shape),
                      head(hd2),
                      pl.BlockSpec((1, seq, hd2), lambda b, h, *_: (b, 0, h)),
                      head(_DA_V_ROWS)],
            out_specs=pl.BlockSpec((1, seq, hd2), lambda b, h, *_: (b, 0, h)),
            scratch_shapes=[pltpu.VMEM((2, t, t), F32), pltpu.VMEM((2, t, t), F32),
                            pltpu.VMEM((2, t, t), BF16), pltpu.VMEM((2, t, t), BF16),
                            pltpu.VMEM((nq, 2, _DA_ACC_ROWS, t), F32)],
        ),
        out_shape=jax.ShapeDtypeStruct((batch, seq, DA_WIDTH), BF16),
        compiler_params=_params("parallel", "parallel"),
        name="diff_attn",
    )(i_tab, j_tab, lq1, lk1, lq2, lk2, g_col, e, mk, qt, k, vt)


def _mem_kv_kernel(m_ref, w_ref, k_ref, v_ref):
    y = _dot(m_ref[0].astype(BF16), w_ref[...])
    k_ref[0] = y[:, :D_MODEL].astype(BF16)
    v_ref[0] = y[:, D_MODEL:].astype(BF16)


def _mem_kv(mem, wkv):
    batch, m, _ = mem.shape
    blk = pl.BlockSpec((1, m, D_MODEL), lambda b: (b, 0, 0))
    out = jax.ShapeDtypeStruct((batch, m, D_MODEL), BF16)
    return pl.pallas_call(
        _mem_kv_kernel,
        grid=(batch,),
        in_specs=[blk, _const_spec(wkv.shape)],
        out_specs=[blk, blk],
        out_shape=[out, out],
        compiler_params=_params("parallel"),
        name="mem_kv",
    )(mem, wkv)


def _mix_cross_kernel(oda_ref, og_ref, h_ref, wo1_ref, wo2_ref, g2_ref, b2_ref,
                      wq_ref, k_ref, v_ref, wo_ref, g3_ref, b3_ref, o_ref):
    tm = h_ref.shape[0]
    groups = [slice(r * tm // MIX_ROW_GROUPS, (r + 1) * tm // MIX_ROW_GROUPS)
              for r in range(MIX_ROW_GROUPS)]
    mix = [_dot(oda_ref[r, :], wo1_ref[...]) + _dot(og_ref[r, :], wo2_ref[...]) for r in groups]
    h2 = [_layer_norm(ALPHA * h_ref[r, :] + m, g2_ref[...], b2_ref[...]) for r, m in zip(groups, mix)]
    q = [(_dot(x.astype(BF16), wq_ref[...]) * (CROSS_HEAD_DIM ** -0.5)).astype(BF16) for x in h2]
    heads = [[] for _ in groups]
    for h in range(CROSS_HEADS):
        cols = slice(h * CROSS_HEAD_DIM, (h + 1) * CROSS_HEAD_DIM)
        for g, qg in enumerate(q):
            s = lax.dot_general(qg[:, cols], k_ref[0, :, cols], _NT, preferred_element_type=F32)
            p = jnp.exp(s - jnp.max(s, axis=-1, keepdims=True))
            l = jnp.sum(p, axis=-1, keepdims=True)
            heads[g].append((_dot(p.astype(BF16), v_ref[0, :, cols]) / l).astype(BF16))
    for r, x, hs in zip(groups, h2, heads):
        c = _dot(jnp.concatenate(hs, axis=-1), wo_ref[...])
        o_ref[r, :] = _layer_norm(ALPHA * x + c, g3_ref[...], b3_ref[...])


def _mix_cross(o_da, o_g, h1, wo1, wo2, g2, b2, wq, k_mem, v_mem, wo, g3, b3, seq):
    n = h1.shape[0]
    tm = MIX_TILE
    per_b = seq // tm
    row = lambda w: pl.BlockSpec((tm, w), lambda i: (i, 0))
    kv = pl.BlockSpec((1,) + k_mem.shape[1:], lambda i: (i // per_b, 0, 0))
    return pl.pallas_call(
        _mix_cross_kernel,
        grid=(n // tm,),
        in_specs=[row(DA_WIDTH), row(GLA_V_WIDTH), row(D_MODEL),
                  _const_spec(wo1.shape), _const_spec(wo2.shape),
                  _const_spec(g2.shape), _const_spec(b2.shape),
                  _const_spec(wq.shape), kv, kv, _const_spec(wo.shape),
                  _const_spec(g3.shape), _const_spec(b3.shape)],
        out_specs=row(D_MODEL),
        out_shape=jax.ShapeDtypeStruct((n, D_MODEL), F32),
        compiler_params=_params("parallel"),
        name="mix_cross",
    )(o_da, o_g, h1, wo1, wo2, g2, b2, wq, k_mem, v_mem, wo, g3, b3)


def kernel(x, mem, ffn1_w_gate, ffn1_w_up, ffn1_w_down, ln1_g, ln1_b, w_in, da_lambda_q1, da_lambda_k1, da_lambda_q2, da_lambda_k2, da_norm_g, gla_w_gate2, gla_b_gate, gla_norm_g, w_out, ln2_g, ln2_b, cross_wq, cross_wkv, cross_wo, ln3_g, ln3_b, ffn2_w_gate, ffn2_w_up, ffn2_w_down, ln4_g, ln4_b):
    batch, seq, d = x.shape
    assert d == D_MODEL and IN_TILE % DA_T == 0 and DA_T % CHUNK == 0
    assert all(seq % tile == 0 for tile in (IN_TILE, MIX_TILE))
    assert (batch * seq) % FFN_TILE == 0 and FFN_TILE % FFN_ROW_GROUPS == 0
    n = batch * seq
    h = x.reshape(n, d)
    row = lambda a, l: a[l].reshape(1, -1)
    for l in range(DEPTH):
        bf = lambda w: w[l].astype(BF16)
        lambda_init = 0.8 - 0.6 * math.exp(-0.3 * l)
        w_in_l = w_in[l]
        pad = LANES - GLA_GATE_RANK
        w_lr = jnp.pad(w_in_l[:, _OFF_GATE:], ((0, 0), (0, pad))).astype(BF16)
        w_g2 = jnp.pad(gla_w_gate2[l], ((0, pad), (0, 0))).astype(BF16)

        h1 = _ffn_ln(h, ffn1_w_gate[l], ffn1_w_up[l], ffn1_w_down[l], row(ln1_g, l), row(ln1_b, l))
        qt, k_da, vt, o_g = _in_proj(
            h1, w_in_l[:, _OFF_DA:_OFF_GLA].astype(BF16), w_in_l[:, _OFF_GLA:_OFF_GATE].astype(BF16),
            w_lr, w_g2, row(gla_b_gate, l), row(gla_norm_g, l), batch, seq)
        o_da = _diff_attn(qt, k_da.reshape(batch, seq, DA_WIDTH), vt,
                          row(da_lambda_q1, l), row(da_lambda_k1, l),
                          row(da_lambda_q2, l), row(da_lambda_k2, l),
                          da_norm_g[l].reshape(-1, 1), lambda_init)
        k_mem, v_mem = _mem_kv(mem, bf(cross_wkv))
        w_out_l = bf(w_out)
        h3 = _mix_cross(o_da.reshape(n, DA_WIDTH), o_g, h1, w_out_l[:DA_WIDTH], w_out_l[DA_WIDTH:],
                        row(ln2_g, l), row(ln2_b, l), bf(cross_wq), k_mem, v_mem, bf(cross_wo),
                        row(ln3_g, l), row(ln3_b, l), seq)
        h = _ffn_ln(h3, ffn2_w_gate[l], ffn2_w_up[l], ffn2_w_down[l], row(ln4_g, l), row(ln4_b, l))
    return h.reshape(batch, seq, d)
```

```python
import functools
import math

import jax
import jax.numpy as jnp
from jax import lax
from jax.experimental import pallas as pl
from jax.experimental.pallas import tpu as pltpu

D_MODEL = 1024
CHUNK = 64
DA_HEADS = 4
DA_HEAD_DIM = 64
DA_WIDTH = DA_HEADS * 2 * DA_HEAD_DIM
GLA_HEADS = 4
GLA_DK = 64
GLA_DV = 128
GLA_K_WIDTH = GLA_HEADS * GLA_DK
GLA_V_WIDTH = GLA_HEADS * GLA_DV
GLA_GATE_RANK = 16
GLA_GATE_NORM = 16.0
FFN_HIDDEN = 2816
CROSS_HEADS = 4
CROSS_HEAD_DIM = D_MODEL // CROSS_HEADS
DEPTH = 1
ALPHA = (2.0 * DEPTH) ** 0.25
EPS = 1e-5
LANES = 128

_OFF_DA = 0
_OFF_GLA = 3 * DA_WIDTH
_GLA_MAIN = 2 * GLA_K_WIDTH + 2 * GLA_V_WIDTH
_OFF_GATE = _OFF_GLA + _GLA_MAIN

MXU_DIM = 256
FFN_SPLITS = (0, FFN_HIDDEN)
assert all(s % MXU_DIM == 0 for s in FFN_SPLITS)
FFN_TILE = 1024
FFN_ROW_GROUPS = 4
FFN_CAST_STEPS = 8
IN_TILE = 1024
MIX_TILE = 1024
MIX_ROW_GROUPS = 4
DA_T = 256
VMEM_LIMIT = 56 * 1024 * 1024

F32 = jnp.float32
BF16 = jnp.bfloat16
_NEG = -1e30
_DA_Q_SCALE = DA_HEAD_DIM ** -0.5 * math.log2(math.e)
_DA_V_PAD = 16
_DA_V_ROWS = 2 * DA_HEAD_DIM + _DA_V_PAD
_DA_ACC_ROWS = 2 * DA_HEAD_DIM + 8
_DA_UNROLL = 34

_NT = (((1,), (1,)), ((), ()))
_TN = (((0,), (0,)), ((), ()))


def _dot(a, b):
    return jnp.dot(a, b, preferred_element_type=F32)


def _split_bf16(a):
    hi = a.astype(BF16)
    lo = (a - hi.astype(F32)).astype(BF16)
    return hi, lo


def _layer_norm(z, g, b):
    mu = jnp.mean(z, axis=-1, keepdims=True)
    zc = z - mu
    var = jnp.mean(zc * zc, axis=-1, keepdims=True)
    return zc * lax.rsqrt(var + EPS) * g + b


def _const_spec(shape):
    nd = len(shape)
    return pl.BlockSpec(shape, lambda *_: (0,) * nd, pipeline_mode=pl.Buffered(1))


def _params(*sem):
    return pltpu.CompilerParams(dimension_semantics=sem, vmem_limit_bytes=VMEM_LIMIT)


def _ffn_ln_kernel(x_ref, wg32_ref, wu32_ref, wd32_ref, g_ref, b_ref, o_ref, wg_ref, wu_ref, wd_ref):
    i = pl.program_id(0)

    @pl.when(i < FFN_CAST_STEPS)
    def _():
        for src, dst in ((wg32_ref, wg_ref), (wu32_ref, wu_ref), (wd32_ref, wd_ref)):
            rows = src.shape[0]
            dst[pl.ds(pl.multiple_of(i * rows, rows), rows), :] = src[...].astype(BF16)

    @pl.when(i >= FFN_CAST_STEPS)
    def _():
        tm = x_ref.shape[0]
        for r in range(FFN_ROW_GROUPS):
            rows = slice(r * tm // FFN_ROW_GROUPS, (r + 1) * tm // FFN_ROW_GROUPS)
            x = x_ref[rows, :]
            xb = x.astype(BF16)
            y = None
            for lo, hi in zip(FFN_SPLITS[:-1], FFN_SPLITS[1:]):
                cols = slice(lo, hi)
                gate = _dot(xb, wg_ref[:, cols])
                up = _dot(xb, wu_ref[:, cols])
                hid = (gate * jax.nn.sigmoid(gate) * up).astype(BF16)
                part = _dot(hid, wd_ref[cols, :])
                y = part if y is None else y + part
            o_ref[rows, :] = _layer_norm(ALPHA * x + 0.5 * y, g_ref[...], b_ref[...])


def _ffn_ln(x, wg, wu, wd, g, b):
    n = x.shape[0]
    tm = FFN_TILE
    nc = FFN_CAST_STEPS
    row = pl.BlockSpec((tm, D_MODEL), lambda i: (jnp.maximum(i - nc, 0), 0))
    w_chunk = lambda w: pl.BlockSpec((w.shape[0] // nc, w.shape[1]), lambda i: (jnp.minimum(i, nc - 1), 0))
    return pl.pallas_call(
        _ffn_ln_kernel,
        grid=(nc + n // tm,),
        in_specs=[row, w_chunk(wg), w_chunk(wu), w_chunk(wd),
                  _const_spec(g.shape), _const_spec(b.shape)],
        out_specs=row,
        out_shape=jax.ShapeDtypeStruct((n, D_MODEL), F32),
        scratch_shapes=[pltpu.VMEM(wg.shape, BF16), pltpu.VMEM(wu.shape, BF16), pltpu.VMEM(wd.shape, BF16)],
        compiler_params=_params("arbitrary"),
        name="ffn_ln",
    )(x, wg, wu, wd, g, b)


def _in_proj_kernel(h_ref, w_ref, wlr_ref, wg2_ref, bg_ref, later_ref, gn_ref,
                    qt_ref, k_ref, vt_ref, og_ref, st_ref, stb_ref, upd_ref, *, tiles_per_batch):
    tm = h_ref.shape[0]
    hd2 = 2 * DA_HEAD_DIM
    t = DA_T
    per_group = t // CHUNK
    sub = lax.broadcasted_iota(jnp.int32, (_DA_V_PAD, t), 0)
    ones_row = jnp.where(sub == 0, 1.0, 0.0).astype(BF16)
    heads_v = [slice(h * GLA_DV, (h + 1) * GLA_DV) for h in range(GLA_HEADS)]
    stacked = (GLA_HEADS * CHUNK, GLA_K_WIDTH)
    own_head = (lax.broadcasted_iota(jnp.int32, stacked, 0) // CHUNK
                == lax.broadcasted_iota(jnp.int32, stacked, 1) // GLA_DK)

    @pl.when(pl.program_id(0) % tiles_per_batch == 0)
    def _():
        st_ref[...] = jnp.zeros_like(st_ref)

    for j in range(tm // t):
        rows = slice(j * t, (j + 1) * t)
        hb = h_ref[rows, :].astype(BF16)

        g_lr = _dot(hb, wlr_ref[...])
        pre = _dot(g_lr.astype(BF16), wg2_ref[...]) + bg_ref[...]
        log_a = (jnp.minimum(pre, 0.0) - jnp.log(1.0 + jnp.exp(-jnp.abs(pre)))) / GLA_GATE_NORM
        la_hi, la_lo = _split_bf16(log_a)
        rev = _dot(later_ref[...], la_hi) + _dot(later_ref[...], la_lo)
        to_end = jnp.exp(rev)
        total = jnp.sum(log_a.reshape(per_group, CHUNK, GLA_K_WIDTH), axis=1)
        decay = jnp.exp(total)

        y_g = _dot(hb, w_ref[:, _OFF_GLA:_OFF_GATE])
        q_g = (y_g[:, :GLA_K_WIDTH] * (GLA_DK ** -0.5)).astype(BF16)
        k_end = (y_g[:, GLA_K_WIDTH:2 * GLA_K_WIDTH] * to_end).astype(BF16)
        v_g = y_g[:, 2 * GLA_K_WIDTH:2 * GLA_K_WIDTH + GLA_V_WIDTH].astype(BF16)
        r_g = y_g[:, 2 * GLA_K_WIDTH + GLA_V_WIDTH:]
        gate = r_g * jax.nn.sigmoid(r_g)

        for cc in range(per_group):
            c, cr = j * per_group + cc, slice(cc * CHUNK, (cc + 1) * CHUNK)
            v_rows = jnp.concatenate([v_g[cr, vr] for vr in heads_v], axis=0)
            k_rows = jnp.where(own_head, jnp.concatenate([k_end[cr]] * GLA_HEADS, axis=0), 0)
            upd_ref[c] = lax.dot_general(v_rows, k_rows, _TN, preferred_element_type=F32)
        for cc in range(per_group):
            c = j * per_group + cc
            new = st_ref[...] * decay[cc:cc + 1, :] + upd_ref[c]
            st_ref[...] = new
            stb_ref[c] = new.astype(BF16)
        for cc in range(per_group):
            c, cr = j * per_group + cc, slice(cc * CHUNK, (cc + 1) * CHUNK)
            q_rows = jnp.where(own_head, jnp.concatenate([q_g[cr]] * GLA_HEADS, axis=0), 0)
            out = lax.dot_general(q_rows, stb_ref[c], _NT, preferred_element_type=F32)
            for h, vr in enumerate(heads_v):
                o = out[h * CHUNK:(h + 1) * CHUNK, :]
                ms = jnp.mean(o * o, axis=-1, keepdims=True)
                og_ref[j * t + cc * CHUNK:j * t + (cc + 1) * CHUNK, vr] = (
                    o * lax.rsqrt(ms + EPS) * gn_ref[...] * gate[cr, vr]).astype(BF16)

        y_da = _dot(hb, w_ref[:, _OFF_DA:_OFF_GLA])
        k_ref[rows, :] = y_da[:, DA_WIDTH:2 * DA_WIDTH].astype(BF16)
        q_t = (y_da[:, :DA_WIDTH] * _DA_Q_SCALE).T
        v_t = y_da[:, 2 * DA_WIDTH:].T
        for h in range(DA_HEADS):
            head = slice(h * hd2, (h + 1) * hd2)
            qt_ref[0, h, j] = q_t[head, :].astype(BF16)
            vt_ref[0, h, j, :hd2, :] = v_t[head, :].astype(BF16)
            vt_ref[0, h, j, hd2:, :] = ones_row


def _in_proj(h, w_main, w_lr, w_g2, b_g, g_norm, batch, seq):
    n = h.shape[0]
    tm = IN_TILE
    per_b = seq // tm
    row = lambda w: pl.BlockSpec((tm, w), lambda i: (i, 0))
    blocked = lambda i: (i // per_b, 0, i % per_b, 0, 0)
    step = jnp.arange(DA_T)
    later = ((step[None, :] > step[:, None])
             & (step[None, :] // CHUNK == step[:, None] // CHUNK)).astype(BF16)
    state = (GLA_DV, GLA_K_WIDTH)
    return pl.pallas_call(
        functools.partial(_in_proj_kernel, tiles_per_batch=per_b),
        grid=(n // tm,),
        in_specs=[row(D_MODEL), _const_spec(w_main.shape),
                  _const_spec(w_lr.shape), _const_spec(w_g2.shape), _const_spec(b_g.shape),
                  _const_spec(later.shape), _const_spec(g_norm.shape)],
        out_specs=[
            pl.BlockSpec((1, DA_HEADS, tm // DA_T, 2 * DA_HEAD_DIM, DA_T), blocked),
            row(DA_WIDTH),
            pl.BlockSpec((1, DA_HEADS, tm // DA_T, _DA_V_ROWS, DA_T), blocked),
            row(GLA_V_WIDTH),
        ],
        out_shape=[
            jax.ShapeDtypeStruct((batch, DA_HEADS, seq // DA_T, 2 * DA_HEAD_DIM, DA_T), BF16),
            jax.ShapeDtypeStruct((n, DA_WIDTH), BF16),
            jax.ShapeDtypeStruct((batch, DA_HEADS, seq // DA_T, _DA_V_ROWS, DA_T), BF16),
            jax.ShapeDtypeStruct((n, GLA_V_WIDTH), BF16),
        ],
        scratch_shapes=[pltpu.VMEM(state, F32),
                        pltpu.VMEM((tm // CHUNK,) + state, BF16),
                        pltpu.VMEM((tm // CHUNK,) + state, F32)],
        compiler_params=_params("arbitrary"),
        name="in_proj",
    )(h, w_main, w_lr, w_g2, b_g, later, g_norm)


def _diff_attn_kernel(it_ref, jt_ref, lq1_ref, lk1_ref, lq2_ref, lk2_ref, g_ref, e_ref, mk_ref,
                      qt_ref, k_ref, vt_ref, o_ref, s_a, s_b, p_a, p_b, acc_ref,
                      *, lambda_init, n_items):
    t = DA_T
    hd2 = 2 * DA_HEAD_DIM
    s_bufs = (s_a, s_b)
    p_bufs = (p_a, p_b)
    row = lax.broadcasted_iota(jnp.int32, (hd2, t), 0)
    halves = (row < DA_HEAD_DIM, row >= DA_HEAD_DIM)

    def scores(x, s_ref):
        i, j = it_ref[x], jt_ref[x]
        kj = k_ref[0, pl.ds(pl.multiple_of(j * t, t), t), :]
        k_ext = jnp.concatenate([kj, e_ref[...]], axis=1)
        qt = qt_ref[0, 0, i]
        diag = jnp.where(i == j, 1.0, 0.0)
        mk = (mk_ref[...] * diag).astype(BF16)
        pad = jnp.zeros((hd2 - mk.shape[0], t), BF16)
        top = []
        for c in range(2):
            rhs = jnp.concatenate([jnp.where(halves[c], qt, 0), mk, pad], axis=0)
            s = _dot(k_ext, rhs)
            s_ref[c] = s
            top.append(jnp.max(s, axis=0, keepdims=True))
        return tuple(top)

    def softmax(x, s_ref, p_ref, m, top):
        first = jt_ref[x] == 0
        m_out, a_out = [], []
        for c in range(2):
            m_old = jnp.where(first, _NEG, m[c])
            m_new = jnp.maximum(m_old, top[c])
            p_ref[c] = jnp.exp2(s_ref[c] - m_new).astype(BF16)
            m_out.append(m_new)
            a_out.append(jnp.exp2(m_old - m_new))
        return tuple(m_out), tuple(a_out)

    def values(x, p_ref, a):
        i, j = it_ref[x], jt_ref[x]
        vtj = vt_ref[0, 0, j]
        for c in range(2):
            acc_ref[i, c] = a[c] * acc_ref[i, c] + _dot(vtj, p_ref[c])[:_DA_ACC_ROWS]

    acc_ref[...] = jnp.zeros_like(acc_ref)
    p_b[...] = jnp.zeros_like(p_b)
    top0 = scores(0, s_a)

    def body(step, carry):
        m, a_prev, top = carry
        for u in range(_DA_UNROLL):
            x = step * _DA_UNROLL + u
            top_next = scores(x + 1, s_bufs[(u + 1) % 2])
            m, a = softmax(x, s_bufs[u % 2], p_bufs[u % 2], m, top)
            values(jnp.maximum(x - 1, 0), p_bufs[(u + 1) % 2], a_prev)
            a_prev, top = a, top_next
        return m, a_prev, top

    neg = jnp.full((1, t), _NEG, F32)
    one = jnp.ones((1, t), F32)
    _, a_last, _ = lax.fori_loop(0, n_items // _DA_UNROLL, body, ((neg, neg), (one, one), top0))
    values(n_items - 1, p_bufs[(n_items - 1) % 2], a_last)

    lam = (jnp.exp(jnp.sum(lq1_ref[...] * lk1_ref[...], axis=-1, keepdims=True))
           - jnp.exp(jnp.sum(lq2_ref[...] * lk2_ref[...], axis=-1, keepdims=True))
           + lambda_init)

    def finish(i, _):
        w0 = 1.0 / acc_ref[i, 0, hd2:hd2 + 1, :]
        w1 = lam / acc_ref[i, 1, hd2:hd2 + 1, :]
        o = acc_ref[i, 0, :hd2, :] * w0 - acc_ref[i, 1, :hd2, :] * w1
        ms = jnp.mean(o * o, axis=0, keepdims=True)
        y = o * (lax.rsqrt(ms + EPS) * (1.0 - lambda_init)) * g_ref[...]
        o_ref[0, pl.ds(pl.multiple_of(i * t, t), t), :] = y.astype(BF16).T
        return 0

    lax.fori_loop(0, acc_ref.shape[0], finish, 0, unroll=2)


def _diff_attn(qt, k, vt, lq1, lk1, lq2, lk2, g_col, lambda_init):
    batch, seq, _ = k.shape
    hd2 = 2 * DA_HEAD_DIM
    t = DA_T
    nq = seq // t
    items = [(i, j) for i in range(nq) for j in range(i + 1)]
    n_items = len(items)
    assert n_items % _DA_UNROLL == 0 and _DA_UNROLL % 2 == 0 and t % CHUNK == 0
    items.append(items[-1])
    i_tab = jnp.asarray([i for i, _ in items], jnp.int32)
    j_tab = jnp.asarray([j for _, j in items], jnp.int32)
    chunk = jnp.arange(t) // CHUNK
    e = (chunk[:, None] == jnp.arange(hd2)[None, :]).astype(BF16)
    ids = jnp.arange(_DA_V_PAD)
    mk = jnp.where((ids[:, None] < t // CHUNK) & (chunk[None, :] < ids[:, None]), _NEG, 0.0).astype(F32)

    const = lambda shape: pl.BlockSpec(shape, lambda *_: (0,) * len(shape), pipeline_mode=pl.Buffered(1))
    head = lambda rows: pl.BlockSpec((1, 1, nq, rows, t), lambda b, h, *_: (b, h, 0, 0, 0))
    return pl.pallas_call(
        functools.partial(_diff_attn_kernel, lambda_init=lambda_init, n_items=n_items),
        grid_spec=pltpu.PrefetchScalarGridSpec(
            num_scalar_prefetch=2,
            grid=(batch, DA_HEADS),
            in_specs=[const(lq1.shape), const(lk1.shape), const(lq2.shape), const(lk2.shape),
                      const(g_col.shape), const(e.shape), const(mk.shape),
                      head(hd2),
                      pl.BlockSpec((1, seq, hd2), lambda b, h, *_: (b, 0, h)),
                      head(_DA_V_ROWS)],
            out_specs=pl.BlockSpec((1, seq, hd2), lambda b, h, *_: (b, 0, h)),
            scratch_shapes=[pltpu.VMEM((2, t, t), F32), pltpu.VMEM((2, t, t), F32),
                            pltpu.VMEM((2, t, t), BF16), pltpu.VMEM((2, t, t), BF16),
                            pltpu.VMEM((nq, 2, _DA_ACC_ROWS, t), F32)],
        ),
        out_shape=jax.ShapeDtypeStruct((batch, seq, DA_WIDTH), BF16),
        compiler_params=_params("parallel", "parallel"),
        name="diff_attn",
    )(i_tab, j_tab, lq1, lk1, lq2, lk2, g_col, e, mk, qt, k, vt)


def _mem_kv_kernel(m_ref, w_ref, k_ref, v_ref):
    y = _dot(m_ref[0].astype(BF16), w_ref[...])
    k_ref[0] = y[:, :D_MODEL].astype(BF16)
    v_ref[0] = y[:, D_MODEL:].astype(BF16)


def _mem_kv(mem, wkv):
    batch, m, _ = mem.shape
    blk = pl.BlockSpec((1, m, D_MODEL), lambda b: (b, 0, 0))
    out = jax.ShapeDtypeStruct((batch, m, D_MODEL), BF16)
    return pl.pallas_call(
        _mem_kv_kernel,
        grid=(batch,),
        in_specs=[blk, _const_spec(wkv.shape)],
        out_specs=[blk, blk],
        out_shape=[out, out],
        compiler_params=_params("parallel"),
        name="mem_kv",
    )(mem, wkv)


def _mix_cross_kernel(oda_ref, og_ref, h_ref, wo1_ref, wo2_ref, g2_ref, b2_ref,
                      wq_ref, k_ref, v_ref, wo_ref, g3_ref, b3_ref, o_ref):
    tm = h_ref.shape[0]
    groups = [slice(r * tm // MIX_ROW_GROUPS, (r + 1) * tm // MIX_ROW_GROUPS)
              for r in range(MIX_ROW_GROUPS)]
    mix = [_dot(oda_ref[r, :], wo1_ref[...]) + _dot(og_ref[r, :], wo2_ref[...]) for r in groups]
    h2 = [_layer_norm(ALPHA * h_ref[r, :] + m, g2_ref[...], b2_ref[...]) for r, m in zip(groups, mix)]
    q = [(_dot(x.astype(BF16), wq_ref[...]) * (CROSS_HEAD_DIM ** -0.5)).astype(BF16) for x in h2]
    heads = [[] for _ in groups]
    for h in range(CROSS_HEADS):
        cols = slice(h * CROSS_HEAD_DIM, (h + 1) * CROSS_HEAD_DIM)
        for g, qg in enumerate(q):
            s = lax.dot_general(qg[:, cols], k_ref[0, :, cols], _NT, preferred_element_type=F32)
            p = jnp.exp(s - jnp.max(s, axis=-1, keepdims=True))
            l = jnp.sum(p, axis=-1, keepdims=True)
            heads[g].append((_dot(p.astype(BF16), v_ref[0, :, cols]) / l).astype(BF16))
    for r, x, hs in zip(groups, h2, heads):
        c = _dot(jnp.concatenate(hs, axis=-1), wo_ref[...])
        o_ref[r, :] = _layer_norm(ALPHA * x + c, g3_ref[...], b3_ref[...])


def _mix_cross(o_da, o_g, h1, wo1, wo2, g2, b2, wq, k_mem, v_mem, wo, g3, b3, seq):
    n = h1.shape[0]
    tm = MIX_TILE
    per_b = seq // tm
    row = lambda w: pl.BlockSpec((tm, w), lambda i: (i, 0))
    kv = pl.BlockSpec((1,) + k_mem.shape[1:], lambda i: (i // per_b, 0, 0))
    return pl.pallas_call(
        _mix_cross_kernel,
        grid=(n // tm,),
        in_specs=[row(DA_WIDTH), row(GLA_V_WIDTH), row(D_MODEL),
                  _const_spec(wo1.shape), _const_spec(wo2.shape),
                  _const_spec(g2.shape), _const_spec(b2.shape),
                  _const_spec(wq.shape), kv, kv, _const_spec(wo.shape),
                  _const_spec(g3.shape), _const_spec(b3.shape)],
        out_specs=row(D_MODEL),
        out_shape=jax.ShapeDtypeStruct((n, D_MODEL), F32),
        compiler_params=_params("parallel"),
        name="mix_cross",
    )(o_da, o_g, h1, wo1, wo2, g2, b2, wq, k_mem, v_mem, wo, g3, b3)


def kernel(x, mem, ffn1_w_gate, ffn1_w_up, ffn1_w_down, ln1_g, ln1_b, w_in, da_lambda_q1, da_lambda_k1, da_lambda_q2, da_lambda_k2, da_norm_g, gla_w_gate2, gla_b_gate, gla_norm_g, w_out, ln2_g, ln2_b, cross_wq, cross_wkv, cross_wo, ln3_g, ln3_b, ffn2_w_gate, ffn2_w_up, ffn2_w_down, ln4_g, ln4_b):
    batch, seq, d = x.shape
    assert d == D_MODEL and IN_TILE % DA_T == 0 and DA_T % CHUNK == 0
    assert all(seq % tile == 0 for tile in (IN_TILE, MIX_TILE))
    assert (batch * seq) % FFN_TILE == 0 and FFN_TILE % FFN_ROW_GROUPS == 0
    n = batch * seq
    h = x.reshape(n, d)
    row = lambda a, l: a[l].reshape(1, -1)
    for l in range(DEPTH):
        bf = lambda w: w[l].astype(BF16)
        lambda_init = 0.8 - 0.6 * math.exp(-0.3 * l)
        w_in_l = w_in[l]
        pad = LANES - GLA_GATE_RANK
        w_lr = jnp.pad(w_in_l[:, _OFF_GATE:], ((0, 0), (0, pad))).astype(BF16)
        w_g2 = jnp.pad(gla_w_gate2[l], ((0, pad), (0, 0))).astype(BF16)

        h1 = _ffn_ln(h, ffn1_w_gate[l], ffn1_w_up[l], ffn1_w_down[l], row(ln1_g, l), row(ln1_b, l))
        qt, k_da, vt, o_g = _in_proj(
            h1, w_in_l[:, :_OFF_GATE].astype(BF16),
            w_lr, w_g2, row(gla_b_gate, l), row(gla_norm_g, l), batch, seq)
        o_da = _diff_attn(qt, k_da.reshape(batch, seq, DA_WIDTH), vt,
                          row(da_lambda_q1, l), row(da_lambda_k1, l),
                          row(da_lambda_q2, l), row(da_lambda_k2, l),
                          da_norm_g[l].reshape(-1, 1), lambda_init)
        k_mem, v_mem = _mem_kv(mem, bf(cross_wkv))
        w_out_l = bf(w_out)
        h3 = _mix_cross(o_da.reshape(n, DA_WIDTH), o_g, h1, w_out_l[:DA_WIDTH], w_out_l[DA_WIDTH:],
                        row(ln2_g, l), row(ln2_b, l), bf(cross_wq), k_mem, v_mem, bf(cross_wo),
                        row(ln3_g, l), row(ln3_b, l), seq)
        h = _ffn_ln(h3, ffn2_w_gate[l], ffn2_w_up[l], ffn2_w_down[l], row(ln4_g, l), row(ln4_b, l))
    return h.reshape(batch, seq, d)
```
